```python
import math
import jax, jax.numpy as jnp
from jax import lax
import numpy as np

D_MODEL = 2048
BATCH = 1
SEQ = 8192
DEPTH = 4

N_META = 16
MIX_WIDTH = D_MODEL
V_DIM = 128
QK_NOPE = 128
QK_ROPE = 64
QK_DIM = QK_NOPE + QK_ROPE
ATT_HEADS = (MIX_WIDTH // 2) // V_DIM
ATT_WIDTH = ATT_HEADS * V_DIM
Q_LORA = 512
KV_LORA = 256
HY_WIDTH = MIX_WIDTH - ATT_WIDTH
HY_ORDER = 2
SHORT_CONV = 3
FILT_EMB = 33
FILT_BANDS = (FILT_EMB - 1) // 2
FILT_ORDER = 64
DECAY_TARGET = 1e-2
FAST_DECAY_PCT = 0.3
SLOW_DECAY_PCT = 1.5
D_FF = -(-8 * D_MODEL // (3 * 256)) * 256
ROPE_THETA = 10000.0
Q_BLOCK = 128
EPS = 1e-6
IN_COLS = Q_LORA + KV_LORA + QK_ROPE + (HY_ORDER + 1) * HY_WIDTH

kernel_name = 'hybrid_mla_hyena_encoder'


def rmsnorm(x, g):
    xf = x.astype(jnp.float32)
    y = xf * lax.rsqrt(jnp.mean(xf * xf, axis=-1, keepdims=True) + EPS)
    return (y * g.astype(jnp.float32)).astype(x.dtype)


def rope_tables(T):
    pos = jnp.arange(T, dtype=jnp.float32)
    inv = ROPE_THETA ** (-jnp.arange(0, QK_ROPE, 2, dtype=jnp.float32) / QK_ROPE)
    ang = pos[:, None] * inv[None, :]
    ang = jnp.concatenate([ang, ang], axis=-1)
    return jnp.cos(ang), jnp.sin(ang)


def apply_rope(x, cos, sin):
    half = QK_ROPE // 2
    x1, x2 = x[..., :half], x[..., half:]
    rot = jnp.concatenate([-x2, x1], axis=-1)
    c = cos[None, :, None, :].astype(x.dtype)
    s = sin[None, :, None, :].astype(x.dtype)
    return x * c + rot * s


def mla(c_q_raw, c_kv_raw, k_rope_raw, q_lat_g, kv_lat_g, w_uq, w_ukv, q_norm_g, k_norm_g, cos, sin):
    B, T, _ = c_q_raw.shape
    q = (rmsnorm(c_q_raw, q_lat_g) @ w_uq).reshape(B, T, ATT_HEADS, QK_DIM)
    kv = (rmsnorm(c_kv_raw, kv_lat_g) @ w_ukv).reshape(B, T, ATT_HEADS, QK_NOPE + V_DIM)
    k_nope, v = kv[..., :QK_NOPE], kv[..., QK_NOPE:]
    k_rope = jnp.broadcast_to(k_rope_raw[:, :, None, :], (B, T, ATT_HEADS, QK_ROPE))
    k = jnp.concatenate([k_nope, k_rope], axis=-1)
    q = rmsnorm(q, q_norm_g)
    k = rmsnorm(k, k_norm_g)
    q = jnp.concatenate([q[..., :QK_NOPE], apply_rope(q[..., QK_NOPE:], cos, sin)], axis=-1)
    k = jnp.concatenate([k[..., :QK_NOPE], apply_rope(k[..., QK_NOPE:], cos, sin)], axis=-1)
    scale = QK_DIM ** -0.5

    def attend(qb):
        s = jnp.einsum('bqhd,bkhd->bhqk', qb, k).astype(jnp.float32) * scale
        p = jax.nn.softmax(s, axis=-1).astype(v.dtype)
        return jnp.einsum('bhqk,bkhd->bqhd', p, v)

    o_meta = attend(q[:, :N_META])
    q_real = q[:, N_META:]
    nb = q_real.shape[1] // Q_BLOCK
    qb = q_real.reshape(B, nb, Q_BLOCK, ATT_HEADS, QK_DIM).transpose(1, 0, 2, 3, 4)
    o_real = lax.map(attend, qb)
    o_real = o_real.transpose(1, 0, 2, 3, 4).reshape(B, nb * Q_BLOCK, ATT_HEADS, V_DIM)
    o = jnp.concatenate([o_meta, o_real], axis=1)
    return o.reshape(B, T, ATT_WIDTH)


def short_conv(u, w, b):
    up = jnp.pad(u, ((0, 0), (1, 1), (0, 0)))
    return up[:, :-2] * w[0] + up[:, 1:-1] * w[1] + up[:, 2:] * w[2] + b


def implicit_filters(L, w1, b1, fr1, w2, b2, fr2, w3):
    f32 = jnp.float32
    t = jnp.linspace(0.0, 1.0, L, dtype=f32)[:, None]
    wpos = 2.0 * math.pi * jnp.arange(L, dtype=f32) / L
    freqs = jnp.linspace(1e-4, FILT_BANDS - 1, FILT_BANDS, dtype=f32)
    ang = wpos[:, None] * freqs[None, :]
    z = jnp.concatenate([t, jnp.cos(ang), -jnp.sin(ang)], axis=-1)
    h = jnp.sin(fr1.astype(f32) * (z @ w1.astype(f32) + b1.astype(f32)))
    h = jnp.sin(fr2.astype(f32) * (h @ w2.astype(f32) + b2.astype(f32)))
    h = h @ w3.astype(f32)
    max_decay = math.log(DECAY_TARGET) / FAST_DECAY_PCT
    min_decay = math.log(DECAY_TARGET) / SLOW_DECAY_PCT
    deltas = jnp.abs(jnp.linspace(min_decay, max_decay, HY_WIDTH, dtype=f32))
    decay = jnp.exp(-t * deltas[None, :])
    return h[:, :HY_WIDTH] * decay, h[:, HY_WIDTH:] * decay


def bidir_long_conv(u, h_f, h_b, d_skip):
    B, L, C = u.shape
    n = 2 * L
    kern = jnp.concatenate([h_f, jnp.zeros((1, C), jnp.float32), h_b[1:][::-1]], axis=0)
    U = jnp.fft.rfft(u.astype(jnp.float32), n=n, axis=1)
    K = jnp.fft.rfft(kern, n=n, axis=0)
    y = jnp.fft.irfft(U * K[None], n=n, axis=1)[:, :L]
    y = y + u.astype(jnp.float32) * d_skip.astype(jnp.float32)
    return y.astype(u.dtype)


def hyena(u_raw, conv_w, conv_b, w1, b1, fr1, w2, b2, fr2, w3, d_skip):
    u = short_conv(u_raw, conv_w, conv_b)
    x0 = u[..., :HY_WIDTH]
    x1 = u[..., HY_WIDTH:2 * HY_WIDTH]
    v = u[..., 2 * HY_WIDTH:]
    h_f, h_b = implicit_filters(u.shape[1], w1, b1, fr1, w2, b2, fr2, w3)
    v = bidir_long_conv(v * x1, h_f, h_b, d_skip)
    return v * x0


def setup_inputs(seed: int = 0) -> dict:
    key = jax.random.key(seed)
    ks = jax.random.split(key, 32)

    def nrm(k, shape, scale):
        return jax.random.normal(k, shape, jnp.float32) * scale

    return {
        'x': nrm(ks[0], (BATCH, SEQ, D_MODEL), 1.0),
        'meta_tokens': nrm(ks[1], (N_META, D_MODEL), 1.0),
        'norm_mix_g': 1.0 + nrm(ks[2], (DEPTH, D_MODEL), 0.02),
        'w_in': nrm(ks[3], (DEPTH, D_MODEL, IN_COLS), D_MODEL ** -0.5),
        'q_lat_g': 1.0 + nrm(ks[4], (DEPTH, Q_LORA), 0.02),
        'kv_lat_g': 1.0 + nrm(ks[5], (DEPTH, KV_LORA), 0.02),
        'w_uq': nrm(ks[6], (DEPTH, Q_LORA, ATT_HEADS * QK_DIM), Q_LORA ** -0.5),
        'w_ukv': nrm(ks[7], (DEPTH, KV_LORA, ATT_HEADS * (QK_NOPE + V_DIM)), KV_LORA ** -0.5),
        'q_norm_g': 1.0 + nrm(ks[8], (DEPTH, QK_DIM), 0.02),
        'k_norm_g': 1.0 + nrm(ks[9], (DEPTH, QK_DIM), 0.02),
        'conv_w': nrm(ks[10], (DEPTH, SHORT_CONV, (HY_ORDER + 1) * HY_WIDTH), SHORT_CONV ** -0.5),
        'conv_b': nrm(ks[11], (DEPTH, (HY_ORDER + 1) * HY_WIDTH), 0.02),
        'filt_w1': nrm(ks[12], (DEPTH, FILT_EMB, FILT_ORDER), FILT_EMB ** -0.5),
        'filt_b1': nrm(ks[13], (DEPTH, FILT_ORDER), 0.1),
        'filt_freq1': 1.0 + nrm(ks[14], (DEPTH, FILT_ORDER), 0.1),
        'filt_w2': nrm(ks[15], (DEPTH, FILT_ORDER, FILT_ORDER), FILT_ORDER ** -0.5),
        'filt_b2': nrm(ks[16], (DEPTH, FILT_ORDER), 0.1),
        'filt_freq2': 1.0 + nrm(ks[17], (DEPTH, FILT_ORDER), 0.1),
        'filt_w3': nrm(ks[18], (DEPTH, FILT_ORDER, 2 * HY_WIDTH), FILT_ORDER ** -0.5),
        'hy_skip': nrm(ks[19], (DEPTH, HY_WIDTH), 1.0),
        'attn_out_g': 1.0 + nrm(ks[20], (DEPTH, ATT_WIDTH), 0.02),
        'hy_out_g': 1.0 + nrm(ks[21], (DEPTH, HY_WIDTH), 0.02),
        'w_out': nrm(ks[22], (DEPTH, MIX_WIDTH, D_MODEL), MIX_WIDTH ** -0.5),
        'norm_ffn_g': 1.0 + nrm(ks[23], (DEPTH, D_MODEL), 0.02),
        'w_gate': nrm(ks[24], (DEPTH, D_MODEL, D_FF), D_MODEL ** -0.5),
        'w_up': nrm(ks[25], (DEPTH, D_MODEL, D_FF), D_MODEL ** -0.5),
        'w_down': nrm(ks[26], (DEPTH, D_FF, D_MODEL), D_FF ** -0.5),
    }


def reference(x, meta_tokens, norm_mix_g, w_in, q_lat_g, kv_lat_g, w_uq, w_ukv, q_norm_g, k_norm_g,
              conv_w, conv_b, filt_w1, filt_b1, filt_freq1, filt_w2, filt_b2, filt_freq2, filt_w3,
              hy_skip, attn_out_g, hy_out_g, w_out, norm_ffn_g, w_gate, w_up, w_down):
    B = x.shape[0]
    meta = jnp.broadcast_to(meta_tokens[None].astype(x.dtype), (B, N_META, D_MODEL))
    h = jnp.concatenate([meta, x], axis=1)
    T = h.shape[1]
    cos, sin = rope_tables(T)
    o1 = Q_LORA
    o2 = o1 + KV_LORA
    o3 = o2 + QK_ROPE
    for l in range(DEPTH):
        hn = rmsnorm(h, norm_mix_g[l])
        p = hn @ w_in[l]
        a = mla(p[..., :o1], p[..., o1:o2], p[..., o2:o3], q_lat_g[l], kv_lat_g[l],
                w_uq[l], w_ukv[l], q_norm_g[l], k_norm_g[l], cos, sin)
        y = hyena(p[..., o3:], conv_w[l], conv_b[l], filt_w1[l], filt_b1[l], filt_freq1[l],
                  filt_w2[l], filt_b2[l], filt_freq2[l], filt_w3[l], hy_skip[l])
        mix = jnp.concatenate([rmsnorm(a, attn_out_g[l]), rmsnorm(y, hy_out_g[l])], axis=-1)
        h = h + mix @ w_out[l]
        hn = rmsnorm(h, norm_ffn_g[l])
        h = h + (jax.nn.silu(hn @ w_gate[l]) * (hn @ w_up[l])) @ w_down[l]
    return h[:, N_META:]
```

```python
import functools
import math

import numpy as np
import jax
import jax.numpy as jnp
from jax import lax
from jax.experimental import pallas as pl
from jax.experimental.pallas import tpu as pltpu

F32 = jnp.float32
BF16 = jnp.bfloat16

D_MODEL = 2048
SEQ = 8192
DEPTH = 4
N_META = 16
T = N_META + SEQ
V_DIM = 128
QK_NOPE = 128
QK_ROPE = 64
QK_DIM = QK_NOPE + QK_ROPE
ATT_HEADS = 8
ATT_WIDTH = ATT_HEADS * V_DIM
Q_LORA = 512
KV_LORA = 256
HY_WIDTH = 1024
FILT_EMB = 33
FILT_BANDS = 16
FILT_ORDER = 64
DECAY_TARGET = 1e-2
FAST_DECAY_PCT = 0.3
SLOW_DECAY_PCT = 1.5
D_FF = 5632
ROPE_THETA = 10000.0
EPS = 1e-6

LANE = 128
HEAD_PAD = 2 * LANE
VMEM_LIMIT = 56 * 1024 * 1024

TP = 8320
TM = 640
TK = 1664
P_COLS = 4096
P_ATT = 1024

N2 = 128
N1 = 130
NFFT = N1 * N2
KH = N1 // 2 + 1
KHP = 72
KC = 6
ZB = 72
TZ = ZB * N2
KB = 136
TKR = KB * N2
NCB = 8
LCH = 4096
TH = 1024


def _cparams(sem):
    return pltpu.CompilerParams(dimension_semantics=sem, vmem_limit_bytes=VMEM_LIMIT)


def _rms(x):
    return lax.rsqrt(jnp.mean(x * x, axis=-1, keepdims=True) + EPS)


def _dot(a, b):
    return jnp.dot(a, b, preferred_element_type=F32)


def _norm_mm_kernel(x_ref, g_ref, w_ref, o_ref, xn_ref):
    @pl.when(pl.program_id(1) == 0)
    def _():
        x = x_ref[...]
        xn_ref[...] = (x * _rms(x) * g_ref[...]).astype(BF16)

    o_ref[...] = _dot(xn_ref[...], w_ref[...])


def norm_matmul(x, g, w, tn):
    m, k = x.shape
    n = w.shape[1]
    return pl.pallas_call(
        _norm_mm_kernel,
        grid=(m // TM, n // tn),
        in_specs=[pl.BlockSpec((TM, k), lambda i, j: (i, 0)),
                  pl.BlockSpec((1, k), lambda i, j: (0, 0)),
                  pl.BlockSpec((k, tn), lambda i, j: (0, j))],
        out_specs=pl.BlockSpec((TM, tn), lambda i, j: (i, j)),
        out_shape=jax.ShapeDtypeStruct((m, n), F32),
        scratch_shapes=[pltpu.VMEM((TM, k), BF16)],
        compiler_params=_cparams(("parallel", "arbitrary")),
        name="norm_matmul",
    )(x, g, w)


def _norm_swiglu_kernel(x_ref, g_ref, wg_ref, wu_ref, o_ref, xn_ref):
    @pl.when(pl.program_id(1) == 0)
    def _():
        x = x_ref[...]
        xn_ref[...] = (x * _rms(x) * g_ref[...]).astype(BF16)

    xn = xn_ref[...]
    a = _dot(xn, wg_ref[...])
    b = _dot(xn, wu_ref[...])
    o_ref[...] = (a * jax.nn.sigmoid(a) * b).astype(BF16)


def norm_swiglu(x, g, wg, wu, tn):
    m, k = x.shape
    n = wg.shape[1]
    return pl.pallas_call(
        _norm_swiglu_kernel,
        grid=(m // TM, n // tn),
        in_specs=[pl.BlockSpec((TM, k), lambda i, j: (i, 0)),
                  pl.BlockSpec((1, k), lambda i, j: (0, 0)),
                  pl.BlockSpec((k, tn), lambda i, j: (0, j)),
                  pl.BlockSpec((k, tn), lambda i, j: (0, j))],
        out_specs=pl.BlockSpec((TM, tn), lambda i, j: (i, j)),
        out_shape=jax.ShapeDtypeStruct((m, n), BF16),
        scratch_shapes=[pltpu.VMEM((TM, k), BF16)],
        compiler_params=_cparams(("parallel", "arbitrary")),
        name="norm_swiglu",
    )(x, g, wg, wu)


def _mm_res_kernel(a_ref, w_ref, r_ref, o_ref):
    o_ref[...] = r_ref[...] + _dot(a_ref[...], w_ref[...])


def matmul_residual(a, w, res, tn):
    m, k = a.shape
    n = w.shape[1]
    return pl.pallas_call(
        _mm_res_kernel,
        grid=(m // TM, n // tn),
        in_specs=[pl.BlockSpec((TM, k), lambda i, j: (i, 0)),
                  pl.BlockSpec((k, tn), lambda i, j: (0, j)),
                  pl.BlockSpec((TM, tn), lambda i, j: (i, j))],
        out_specs=pl.BlockSpec((TM, tn), lambda i, j: (i, j)),
        out_shape=jax.ShapeDtypeStruct((m, n), F32),
        compiler_params=_cparams(("parallel", "arbitrary")),
        name="matmul_residual",
    )(a, w, res)


def _mix_mm_res_kernel(a_ref, y_ref, ga_ref, gy_ref, w_ref, r_ref, o_ref, mix_ref):
    @pl.when(pl.program_id(1) == 0)
    def _():
        a = a_ref[...]
        mix_ref[:, :ATT_WIDTH] = (a * _rms(a) * ga_ref[...]).astype(BF16)
        y = jnp.concatenate([y_ref[c] for c in range(NCB)], axis=-1)
        mix_ref[:, ATT_WIDTH:] = (y * _rms(y) * gy_ref[...]).astype(BF16)

    o_ref[...] = r_ref[...] + _dot(mix_ref[...], w_ref[...])


def mix_matmul_residual(a, yh, ga, gy, w, res, tn):
    m = a.shape[0]
    n = w.shape[1]
    return pl.pallas_call(
        _mix_mm_res_kernel,
        grid=(m // TM, n // tn),
        in_specs=[pl.BlockSpec((TM, ATT_WIDTH), lambda i, j: (i, 0)),
                  pl.BlockSpec((NCB, TM, LANE), lambda i, j: (0, i, 0)),
                  pl.BlockSpec((1, ATT_WIDTH), lambda i, j: (0, 0)),
                  pl.BlockSpec((1, HY_WIDTH), lambda i, j: (0, 0)),
                  pl.BlockSpec((ATT_WIDTH + HY_WIDTH, tn), lambda i, j: (0, j)),
                  pl.BlockSpec((TM, tn), lambda i, j: (i, j))],
        out_specs=pl.BlockSpec((TM, tn), lambda i, j: (i, j)),
        out_shape=jax.ShapeDtypeStruct((m, n), F32),
        scratch_shapes=[pltpu.VMEM((TM, ATT_WIDTH + HY_WIDTH), BF16)],
        compiler_params=_cparams(("parallel", "arbitrary")),
        name="mix_matmul_residual",
    )(a, yh, ga, gy, w, res)


def _mla_prep_kernel(p_ref, cos_ref, sin_ref, glq_ref, glkv_ref, wq_ref, wkv_ref, gq_ref, gk_ref,
                     q_ref, k_ref, v_ref):
    p = p_ref[...]
    cq = p[:, :Q_LORA]
    ckv = p[:, Q_LORA:Q_LORA + KV_LORA]
    kr = p[:, Q_LORA + KV_LORA:Q_LORA + KV_LORA + LANE]
    qp = _dot((cq * _rms(cq) * glq_ref[...]).astype(BF16), wq_ref[...])
    kvp = _dot((ckv * _rms(ckv) * glkv_ref[...]).astype(BF16), wkv_ref[...])
    cosp = cos_ref[...]
    sinp = sin_ref[...]
    rope_lane = lax.broadcasted_iota(jnp.int32, (1, LANE), 1) < QK_ROPE
    gq = gq_ref[...]
    gk = gk_ref[...]
    q_cos, q_sin = cosp * gq[1:2], sinp * gq[2:3]
    k_cos, k_sin = cosp * gk[1:2], sinp * gk[2:3]
    kr_ss = jnp.sum(jnp.where(rope_lane, kr * kr, 0.0), axis=-1, keepdims=True)
    kr_rot = kr * k_cos + pltpu.roll(kr, QK_ROPE, 1) * k_sin
    q_scale = QK_DIM ** -0.5 * math.log2(math.e)
    for h in range(ATT_HEADS):
        qn = qp[:, h * HEAD_PAD:h * HEAD_PAD + LANE]
        qa = qp[:, h * HEAD_PAD + LANE:(h + 1) * HEAD_PAD]
        ss = (jnp.sum(qn * qn, axis=-1, keepdims=True)
              + jnp.sum(jnp.where(rope_lane, qa * qa, 0.0), axis=-1, keepdims=True))
        rq = lax.rsqrt(ss * (1.0 / QK_DIM) + EPS) * q_scale
        q_ref[h, :, :LANE] = (qn * rq * gq[0:1]).astype(BF16)
        q_ref[h, :, LANE:] = ((qa * q_cos + pltpu.roll(qa, QK_ROPE, 1) * q_sin) * rq).astype(BF16)
        kn = kvp[:, h * HEAD_PAD:h * HEAD_PAD + LANE]
        rk = lax.rsqrt((jnp.sum(kn * kn, axis=-1, keepdims=True) + kr_ss) * (1.0 / QK_DIM) + EPS)
        k_ref[h, :, :LANE] = (kn * rk * gk[0:1]).astype(BF16)
        k_ref[h, :, LANE:] = (kr_rot * rk).astype(BF16)
        v_ref[h] = kvp[:, h * HEAD_PAD + LANE:(h + 1) * HEAD_PAD].astype(BF16)


def mla_prep(p, cosp, sinp, glq, glkv, wq, wkv, gq, gk):
    const = lambda i: (0, 0)
    return pl.pallas_call(
        _mla_prep_kernel,
        grid=(TP // TM,),
        in_specs=[pl.BlockSpec((TM, P_ATT), lambda i: (i, 0)),
                  pl.BlockSpec((TM, LANE), lambda i: (i, 0)),
                  pl.BlockSpec((TM, LANE), lambda i: (i, 0)),
                  pl.BlockSpec((1, Q_LORA), const),
                  pl.BlockSpec((1, KV_LORA), const),
                  pl.BlockSpec((Q_LORA, ATT_HEADS * HEAD_PAD), const),
                  pl.BlockSpec((KV_LORA, ATT_HEADS * HEAD_PAD), const),
                  pl.BlockSpec((3, LANE), const),
                  pl.BlockSpec((3, LANE), const)],
        out_specs=[pl.BlockSpec((ATT_HEADS, TM, HEAD_PAD), lambda i: (0, i, 0)),
                   pl.BlockSpec((ATT_HEADS, TM, HEAD_PAD), lambda i: (0, i, 0)),
                   pl.BlockSpec((ATT_HEADS, TM, V_DIM), lambda i: (0, i, 0))],
        out_shape=[jax.ShapeDtypeStruct((ATT_HEADS, TP, HEAD_PAD), BF16),
                   jax.ShapeDtypeStruct((ATT_HEADS, TP, HEAD_PAD), BF16),
                   jax.ShapeDtypeStruct((ATT_HEADS, TP, V_DIM), BF16)],
        compiler_params=_cparams(("parallel",)),
        name="mla_prep",
    )(p, cosp, sinp, glq, glkv, wq, wkv, gq, gk)


def _attn_kernel(q_ref, k_ref, v_ref, o_ref):
    q = q_ref[0]
    m = jnp.full((TM, 1), -jnp.inf, F32)
    l = jnp.zeros((TM, 1), F32)
    acc = jnp.zeros((TM, V_DIM), F32)
    for kb in range(TP // TK):
        k = k_ref[0, kb * TK:(kb + 1) * TK, :]
        s = lax.dot_general(q, k, (((1,), (1,)), ((), ())), preferred_element_type=F32)
        if (kb + 1) * TK > T:
            col = kb * TK + lax.broadcasted_iota(jnp.int32, (1, TK), 1)
            s = jnp.where(col < T, s, -jnp.inf)
        m_new = jnp.maximum(m, jnp.max(s, axis=-1, keepdims=True))
        alpha = jnp.exp2(m - m_new)
        pr = jnp.exp2(s - m_new)
        l = alpha * l + jnp.sum(pr, axis=-1, keepdims=True)
        acc = alpha * acc + _dot(pr.astype(BF16), v_ref[0, kb * TK:(kb + 1) * TK, :])
        m = m_new
    o_ref[...] = acc / l


def attention(q, k, v):
    return pl.pallas_call(
        _attn_kernel,
        grid=(ATT_HEADS, TP // TM),
        in_specs=[pl.BlockSpec((1, TM, HEAD_PAD), lambda h, i: (h, i, 0)),
                  pl.BlockSpec((1, TP, HEAD_PAD), lambda h, i: (h, 0, 0)),
                  pl.BlockSpec((1, TP, V_DIM), lambda h, i: (h, 0, 0))],
        out_specs=pl.BlockSpec((TM, V_DIM), lambda h, i: (i, h)),
        out_shape=jax.ShapeDtypeStruct((TP, ATT_WIDTH), F32),
        compiler_params=_cparams(("parallel", "parallel")),
        name="attention",
    )(q, k, v)


def _short_conv(x, prev_row, next_row, w, b, t0):
    rows = x.shape[0]
    t = t0 + lax.broadcasted_iota(jnp.int32, (rows, 1), 0)
    r = lax.broadcasted_iota(jnp.int32, (rows, 1), 0)
    xm = jnp.where(r == 0, prev_row, pltpu.roll(x, 1, 0))
    xm = jnp.where(t == 0, 0.0, xm)
    xp = jnp.where(r == rows - 1, next_row, pltpu.roll(x, rows - 1, 0))
    xp = jnp.where(t >= T - 1, 0.0, xp)
    return xm * w[0:1] + x * w[1:2] + xp * w[2:3] + b


def _hy_pre_kernel(x0_ref, x1_ref, v_ref, x0p_ref, x1p_ref, vp_ref, x0n_ref, x1n_ref, vn_ref,
                   w0_ref, w1_ref, wv_ref, b0_ref, b1_ref, bv_ref, z_ref, g_ref):
    t0 = pl.program_id(0) * TH
    valid = (t0 + lax.broadcasted_iota(jnp.int32, (TH, 1), 0)) < T
    u0 = _short_conv(x0_ref[...], x0p_ref[7:8], x0n_ref[0:1], w0_ref[...], b0_ref[...], t0)
    u1 = _short_conv(x1_ref[...], x1p_ref[7:8], x1n_ref[0:1], w1_ref[...], b1_ref[...], t0)
    uv = _short_conv(v_ref[...], vp_ref[7:8], vn_ref[0:1], wv_ref[...], bv_ref[...], t0)
    z = jnp.where(valid, uv * u1, 0.0)
    g = jnp.where(valid, u0, 0.0)
    ncb = z_ref.shape[0]
    for c in range(ncb):
        z_ref[c] = z[:, c * LANE:(c + 1) * LANE]
        g_ref[c] = g[:, c * LANE:(c + 1) * LANE]


def hyena_pre(p, conv_w, conv_b):
    cw = 512
    ncw = HY_WIDTH // cw
    off = P_ATT // cw
    halo = TH // 8
    last8 = TP // 8 - 1

    def main(s):
        return pl.BlockSpec((TH, cw), lambda i, j: (i, off + s * ncw + j))

    def prev(s):
        return pl.BlockSpec((8, cw), lambda i, j: (jnp.maximum(i * halo - 1, 0), off + s * ncw + j))

    def nxt(s):
        return pl.BlockSpec((8, cw), lambda i, j: (jnp.minimum((i + 1) * halo, last8), off + s * ncw + j))

    def par(s, rows):
        return pl.BlockSpec((rows, cw), lambda i, j: (0, s * ncw + j))

    out_spec = pl.BlockSpec((cw // LANE, TH, LANE), lambda i, j: (j, i, 0))
    out_sds = jax.ShapeDtypeStruct((NCB, TZ, LANE), F32)
    return pl.pallas_call(
        _hy_pre_kernel,
        grid=(TZ // TH, ncw),
        in_specs=[main(0), main(1), main(2), prev(0), prev(1), prev(2), nxt(0), nxt(1), nxt(2),
                  par(0, 3), par(1, 3), par(2, 3), par(0, 1), par(1, 1), par(2, 1)],
        out_specs=[out_spec, out_spec],
        out_shape=[out_sds, out_sds],
        compiler_params=_cparams(("parallel", "parallel")),
        name="hyena_pre",
    )(p, p, p, p, p, p, p, p, p, conv_w, conv_w, conv_w, conv_b, conv_b, conv_b)


def _filter_kernel(zf_ref, aux_ref, w1_ref, b1_ref, f1_ref, w2_ref, b2_ref, f2_ref, w3_ref, dl_ref, o_ref):
    hi = lax.Precision.HIGHEST
    h = jnp.sin(f1_ref[...] * (jnp.dot(zf_ref[...], w1_ref[...], precision=hi,
                                       preferred_element_type=F32) + b1_ref[...]))
    h = jnp.sin(f2_ref[...] * (jnp.dot(h, w2_ref[...], precision=hi,
                                       preferred_element_type=F32) + b2_ref[...]))
    hh = jnp.dot(h, w3_ref[...], precision=hi, preferred_element_type=F32)
    aux = aux_ref[...]
    decay = jnp.exp(-aux[:, 0:1] * dl_ref[...])
    kern = (hh[:, :HY_WIDTH] * aux[:, 1:2] + hh[:, HY_WIDTH:] * aux[:, 2:3]) * decay
    for c in range(NCB):
        o_ref[c] = kern[:, c * LANE:(c + 1) * LANE]


def hyena_filter(zf, aux, w1, b1, f1, w2, b2, f2, w3, deltas):
    const = lambda i: (0, 0)
    return pl.pallas_call(
        _filter_kernel,
        grid=(TKR // TH,),
        in_specs=[pl.BlockSpec((TH, LANE), lambda i: (i, 0)),
                  pl.BlockSpec((TH, 8), lambda i: (i, 0)),
                  pl.BlockSpec((LANE, LANE), const), pl.BlockSpec((1, LANE), const),
                  pl.BlockSpec((1, LANE), const),
                  pl.BlockSpec((LANE, LANE), const), pl.BlockSpec((1, LANE), const),
                  pl.BlockSpec((1, LANE), const),
                  pl.BlockSpec((LANE, 2 * HY_WIDTH), const),
                  pl.BlockSpec((1, HY_WIDTH), const)],
        out_specs=pl.BlockSpec((NCB, TH, LANE), lambda i: (0, i, 0)),
        out_shape=jax.ShapeDtypeStruct((NCB, TKR, LANE), F32),
        compiler_params=_cparams(("parallel",)),
        name="hyena_filter",
    )(zf, aux, w1, b1, f1, w2, b2, f2, w3, deltas)


def _dft_a_kernel(f_ref, x_ref, o_ref):
    x = x_ref[0]
    kpad = f_ref.shape[1] - x.shape[0]
    x = jnp.concatenate([x, jnp.zeros((kpad, x.shape[1]), F32)], axis=0).astype(BF16)
    o_ref[0] = _dot(f_ref[...], x).astype(BF16)


def dft_stage_a(f, x):
    blocks = x.shape[1]
    width = x.shape[2]
    return pl.pallas_call(
        _dft_a_kernel,
        grid=(NCB, width // LCH),
        in_specs=[pl.BlockSpec(f.shape, lambda c, j: (0, 0)),
                  pl.BlockSpec((1, blocks, LCH), lambda c, j: (c, 0, j))],
        out_specs=pl.BlockSpec((1, 2 * KHP, LCH), lambda c, j: (c, 0, j)),
        out_shape=jax.ShapeDtypeStruct((NCB, 2 * KHP, width), BF16),
        compiler_params=_cparams(("parallel", "parallel")),
        name="dft_stage_a",
    )(f, x)


def _dft_b_kernel(zr_ref, zi_ref, kr_ref, ki_ref, fb_ref, gb_ref, br_ref, bi_ref):
    for k in range(KC):
        rows = slice(k * N2, (k + 1) * N2)
        fb = fb_ref[k]
        xs = _dot(fb, jnp.concatenate([zr_ref[0, rows, :], zi_ref[0, rows, :]], axis=0))
        ks = _dot(fb, jnp.concatenate([kr_ref[0, rows, :], ki_ref[0, rows, :]], axis=0))
        xr, xi = xs[:N2], xs[N2:]
        fr, fi = ks[:N2], ks[N2:]
        y = jnp.concatenate([xr * fr - xi * fi, xr * fi + xi * fr], axis=0).astype(BF16)
        b = _dot(gb_ref[k], y)
        br_ref[0, rows, :] = b[:N2]
        bi_ref[0, rows, :] = b[N2:]


def dft_stage_b(az, ak, fb, gb):
    im_off = KHP // KC
    re_spec = pl.BlockSpec((1, KC * N2, LANE), lambda k, c: (c, k, 0))
    im_spec = pl.BlockSpec((1, KC * N2, LANE), lambda k, c: (c, im_off + k, 0))
    mat_spec = pl.BlockSpec((KC, 2 * N2, 2 * N2), lambda k, c: (k, 0, 0))
    out_sds = jax.ShapeDtypeStruct((NCB, KHP * N2, LANE), F32)
    return pl.pallas_call(
        _dft_b_kernel,
        grid=(KHP // KC, NCB),
        in_specs=[re_spec, im_spec, re_spec, im_spec, mat_spec, mat_spec],
        out_specs=[re_spec, re_spec],
        out_shape=[out_sds, out_sds],
        compiler_params=_cparams(("parallel", "parallel")),
        name="dft_stage_b",
    )(az, az, ak, ak, fb, gb)


def _dft_a_inv_kernel(gr_ref, gi_ref, br_ref, bi_ref, z_ref, g_ref, d_ref, o_ref):
    kpad = gr_ref.shape[1] - KHP
    pad = jnp.zeros((kpad, LCH), F32)
    br = jnp.concatenate([br_ref[0], pad], axis=0).astype(BF16)
    bi = jnp.concatenate([bi_ref[0], pad], axis=0).astype(BF16)
    y = _dot(gr_ref[...], br) + _dot(gi_ref[...], bi)
    o_ref[0] = (y + z_ref[0] * d_ref[0]) * g_ref[0]


def dft_stage_a_inv(gr, gi, br, bi, z, g, d):
    blk = pl.BlockSpec((1, ZB, LCH), lambda c, j: (c, 0, j))
    mat = pl.BlockSpec(gr.shape, lambda c, j: (0, 0))
    return pl.pallas_call(
        _dft_a_inv_kernel,
        grid=(NCB, N2 * LANE // LCH),
        in_specs=[mat, mat, blk, blk, blk, blk, pl.BlockSpec((1, 1, LCH), lambda c, j: (c, 0, j))],
        out_specs=blk,
        out_shape=jax.ShapeDtypeStruct((NCB, ZB, N2 * LANE), F32),
        compiler_params=_cparams(("parallel", "parallel")),
        name="dft_stage_a_inv",
    )(gr, gi, br, bi, z, g, d)


@functools.lru_cache(maxsize=None)
def _dft_constants():
    k1 = np.arange(KHP, dtype=np.int64)
    live_k1 = (k1 < KH).astype(np.float64)

    def stage_a(nblocks, kdim):
        n1 = np.arange(kdim, dtype=np.int64)
        ang = 2.0 * np.pi * ((k1[:, None] * n1[None, :]) % N1) / N1
        live = live_k1[:, None] * (n1 < nblocks)[None, :]
        return np.concatenate([np.cos(ang) * live, -np.sin(ang) * live], axis=0).astype(np.float32)

    fa_data = stage_a(-(-T // N2), LANE)
    fa_filt = stage_a(N1, 2 * LANE)

    n2 = np.arange(N2, dtype=np.int64)
    k2 = np.arange(N2, dtype=np.int64)
    freq = k1[:, None, None] + N1 * k2[None, :, None]
    theta = 2.0 * np.pi * ((freq * n2[None, None, :]) % NFFT) / NFFT
    c = np.cos(theta) * live_k1[:, None, None]
    s = np.sin(theta) * live_k1[:, None, None]
    fb = np.concatenate([np.concatenate([c, s], axis=2), np.concatenate([-s, c], axis=2)], axis=1)
    gb = np.transpose(fb, (0, 2, 1))

    n1 = np.arange(ZB, dtype=np.int64)
    kk = np.arange(LANE, dtype=np.int64)
    herm = np.where((kk == 0) | (kk == N1 // 2), 1.0, 2.0) * (kk < KH) / NFFT
    phi = 2.0 * np.pi * ((n1[:, None] * kk[None, :]) % N1) / N1
    live_n1 = (n1 < -(-T // N2)).astype(np.float64)[:, None]
    gr = (np.cos(phi) * herm[None, :] * live_n1).astype(np.float32)
    gi = (-np.sin(phi) * herm[None, :] * live_n1).astype(np.float32)
    return fa_data, fa_filt, fb.astype(np.float32), gb.astype(np.float32), gr, gi


def _filter_positions():
    t = np.arange(TKR)
    fwd = t < T
    bwd = (t > NFFT - T) & (t < NFFT)
    pos = np.where(fwd, t, np.where(bwd, NFFT - t, 0))
    return pos, fwd.astype(np.float32), bwd.astype(np.float32)


def _rope_tables():
    pos = jnp.arange(TP, dtype=F32)
    inv = ROPE_THETA ** (-jnp.arange(0, QK_ROPE, 2, dtype=F32) / QK_ROPE)
    ang = pos[:, None] * inv[None, :]
    ang = jnp.concatenate([ang, ang], axis=-1)
    pad = jnp.zeros((TP, LANE - QK_ROPE), F32)
    return jnp.concatenate([jnp.cos(ang), pad], axis=-1), jnp.concatenate([jnp.sin(ang), pad], axis=-1)


def _rot_cols(w):
    half = QK_ROPE // 2
    return jnp.concatenate([-w[..., half:], w[..., :half]], axis=-1)


def _rope_gain_rows(g):
    half = QK_ROPE // 2
    gr = g[QK_NOPE:]
    pad = jnp.zeros((LANE - QK_ROPE,), F32)
    return jnp.stack([g[:QK_NOPE], jnp.concatenate([gr, pad]),
                      jnp.concatenate([gr[half:], gr[:half], pad])])


def kernel(x, meta_tokens, norm_mix_g, w_in, q_lat_g, kv_lat_g, w_uq, w_ukv, q_norm_g, k_norm_g, conv_w, conv_b, filt_w1, filt_b1, filt_freq1, filt_w2, filt_b2, filt_freq2, filt_w3, hy_skip, attn_out_g, hy_out_g, w_out, norm_ffn_g, w_gate, w_up, w_down):
    assert x.shape == (1, SEQ, D_MODEL)
    h = jnp.concatenate([meta_tokens.astype(F32), x[0], jnp.zeros((TP - T, D_MODEL), F32)], axis=0)

    cosp, sinp = _rope_tables()
    fa_data, fa_filt, fb, gb, gr, gi = (jnp.asarray(m).astype(BF16) for m in _dft_constants())

    pos, fwd, bwd = _filter_positions()
    tl = jnp.linspace(0.0, 1.0, T, dtype=F32)[:, None]
    wpos = 2.0 * math.pi * jnp.arange(T, dtype=F32) / T
    ang = wpos[:, None] * jnp.linspace(1e-4, FILT_BANDS - 1, FILT_BANDS, dtype=F32)[None, :]
    feats = jnp.concatenate([tl, jnp.cos(ang), -jnp.sin(ang)], axis=-1)[pos]
    zf = jnp.pad(feats, ((0, 0), (0, LANE - FILT_EMB)))
    aux = jnp.pad(jnp.stack([tl[pos, 0], jnp.asarray(fwd), jnp.asarray(bwd)], axis=-1), ((0, 0), (0, 5)))
    max_decay = math.log(DECAY_TARGET) / FAST_DECAY_PCT
    min_decay = math.log(DECAY_TARGET) / SLOW_DECAY_PCT
    deltas = jnp.abs(jnp.linspace(min_decay, max_decay, HY_WIDTH, dtype=F32))[None, :]
    opad = LANE - FILT_ORDER

    o1, o2, o3 = Q_LORA, Q_LORA + KV_LORA, Q_LORA + KV_LORA + QK_ROPE
    for l in range(DEPTH):
        w_rope = w_in[l][:, o2:o3]
        w_in_big = jnp.concatenate(
            [w_in[l][:, :o3], _rot_cols(w_rope), jnp.zeros((D_MODEL, P_ATT - o3 - QK_ROPE), F32),
             w_in[l][:, o3:]], axis=1).astype(BF16)
        wq = w_uq[l].reshape(Q_LORA, ATT_HEADS, QK_DIM)
        wq = jnp.concatenate([wq, _rot_cols(wq[..., QK_NOPE:])], axis=-1)
        wq = wq.reshape(Q_LORA, ATT_HEADS * HEAD_PAD).astype(BF16)
        wkv = w_ukv[l].astype(BF16)

        p = norm_matmul(h, norm_mix_g[l][None], w_in_big, 1024)
        q, k, v = mla_prep(p, cosp, sinp, q_lat_g[l][None], kv_lat_g[l][None], wq, wkv,
                           _rope_gain_rows(q_norm_g[l]), _rope_gain_rows(k_norm_g[l]))
        att = attention(q, k, v)

        z, gate = hyena_pre(p, conv_w[l], conv_b[l][None])
        kern = hyena_filter(
            zf, aux,
            jnp.pad(filt_w1[l], ((0, LANE - FILT_EMB), (0, opad))), jnp.pad(filt_b1[l], (0, opad))[None],
            jnp.pad(filt_freq1[l], (0, opad))[None],
            jnp.pad(filt_w2[l], ((0, opad), (0, opad))), jnp.pad(filt_b2[l], (0, opad))[None],
            jnp.pad(filt_freq2[l], (0, opad))[None],
            jnp.pad(filt_w3[l], ((0, opad), (0, 0))), deltas)
        zl = z.reshape(NCB, ZB, N2 * LANE)
        az = dft_stage_a(fa_data, zl).reshape(NCB, 2 * KHP * N2, LANE)
        ak = dft_stage_a(fa_filt, kern.reshape(NCB, KB, N2 * LANE)).reshape(NCB, 2 * KHP * N2, LANE)
        br, bi = dft_stage_b(az, ak, fb, gb)
        skip = jnp.tile(hy_skip[l].reshape(NCB, 1, LANE), (1, 1, N2))
        yh = dft_stage_a_inv(gr, gi, br.reshape(NCB, KHP, N2 * LANE), bi.reshape(NCB, KHP, N2 * LANE),
                             zl, gate.reshape(NCB, ZB, N2 * LANE), skip)
        yh = yh.reshape(NCB, TZ, LANE)

        h = mix_matmul_residual(att, yh, attn_out_g[l][None], hy_out_g[l][None],
                                w_out[l].astype(BF16), h, 1024)

        act = norm_swiglu(h, norm_ffn_g[l][None], w_gate[l].astype(BF16), w_up[l].astype(BF16), 512)
        h = matmul_residual(act, w_down[l].astype(BF16), h, 1024)

    return h[N_META:T][None]
```

```python
import functools
import math

import numpy as np
import jax
import jax.numpy as jnp
from jax import lax
from jax.experimental import pallas as pl
from jax.experimental.pallas import tpu as pltpu

F32 = jnp.float32
BF16 = jnp.bfloat16

D_MODEL = 2048
SEQ = 8192
DEPTH = 4
N_META = 16
T = N_META + SEQ
V_DIM = 128
QK_NOPE = 128
QK_ROPE = 64
QK_DIM = QK_NOPE + QK_ROPE
ATT_HEADS = 8
ATT_WIDTH = ATT_HEADS * V_DIM
Q_LORA = 512
KV_LORA = 256
HY_WIDTH = 1024
FILT_EMB = 33
FILT_BANDS = 16
FILT_ORDER = 64
DECAY_TARGET = 1e-2
FAST_DECAY_PCT = 0.3
SLOW_DECAY_PCT = 1.5
D_FF = 5632
ROPE_THETA = 10000.0
EPS = 1e-6

LANE = 128
HEAD_PAD = 2 * LANE
VMEM_LIMIT = 56 * 1024 * 1024

TP = 8320
TM = 640
TQ = 1040
TK = 1664
P_COLS = 4096
P_ATT = 1024
TH = 1024

N2 = 128
N1 = 130
NFFT = N1 * N2
KH = N1 // 2 + 1
KC = 6
NKC = KH // KC
NB = TP // N2
NBP = 72
NCB = HY_WIDTH // LANE
PZ = 136
PA = 136
IM0 = 68
PB = 264


def _cparams(sem):
    return pltpu.CompilerParams(dimension_semantics=sem, vmem_limit_bytes=VMEM_LIMIT)


def _rms(x):
    return lax.rsqrt(jnp.mean(x * x, axis=-1, keepdims=True) + EPS)


def _dot(a, b):
    return jnp.dot(a, b, preferred_element_type=F32)


def _norm_mm_kernel(x_ref, g_ref, w_ref, o_ref, xn_ref):
    @pl.when(pl.program_id(1) == 0)
    def _():
        x = x_ref[...]
        xn_ref[...] = (x * _rms(x) * g_ref[...]).astype(BF16)

    o_ref[...] = _dot(xn_ref[...], w_ref[...])


def norm_matmul(x, g, w, tn):
    m, k = x.shape
    n = w.shape[1]
    return pl.pallas_call(
        _norm_mm_kernel,
        grid=(m // TM, n // tn),
        in_specs=[pl.BlockSpec((TM, k), lambda i, j: (i, 0)),
                  pl.BlockSpec((1, k), lambda i, j: (0, 0)),
                  pl.BlockSpec((k, tn), lambda i, j: (0, j))],
        out_specs=pl.BlockSpec((TM, tn), lambda i, j: (i, j)),
        out_shape=jax.ShapeDtypeStruct((m, n), F32),
        scratch_shapes=[pltpu.VMEM((TM, k), BF16)],
        compiler_params=_cparams(("parallel", "arbitrary")),
        name="norm_matmul",
    )(x, g, w)


def _norm_swiglu_kernel(x_ref, g_ref, wg_ref, wu_ref, o_ref, xn_ref):
    @pl.when(pl.program_id(1) == 0)
    def _():
        x = x_ref[...]
        xn_ref[...] = (x * _rms(x) * g_ref[...]).astype(BF16)

    xn = xn_ref[...]
    a = _dot(xn, wg_ref[...])
    b = _dot(xn, wu_ref[...])
    o_ref[...] = (a * jax.nn.sigmoid(a) * b).astype(BF16)


def norm_swiglu(x, g, wg, wu, tn):
    m, k = x.shape
    n = wg.shape[1]
    return pl.pallas_call(
        _norm_swiglu_kernel,
        grid=(m // TM, n // tn),
        in_specs=[pl.BlockSpec((TM, k), lambda i, j: (i, 0)),
                  pl.BlockSpec((1, k), lambda i, j: (0, 0)),
                  pl.BlockSpec((k, tn), lambda i, j: (0, j)),
                  pl.BlockSpec((k, tn), lambda i, j: (0, j))],
        out_specs=pl.BlockSpec((TM, tn), lambda i, j: (i, j)),
        out_shape=jax.ShapeDtypeStruct((m, n), BF16),
        scratch_shapes=[pltpu.VMEM((TM, k), BF16)],
        compiler_params=_cparams(("parallel", "arbitrary")),
        name="norm_swiglu",
    )(x, g, wg, wu)


def _mm_res_kernel(a_ref, w_ref, r_ref, o_ref):
    o_ref[...] = r_ref[...] + _dot(a_ref[...], w_ref[...])


def matmul_residual(a, w, res, tn):
    m, k = a.shape
    n = w.shape[1]
    return pl.pallas_call(
        _mm_res_kernel,
        grid=(m // TM, n // tn),
        in_specs=[pl.BlockSpec((TM, k), lambda i, j: (i, 0)),
                  pl.BlockSpec((k, tn), lambda i, j: (0, j)),
                  pl.BlockSpec((TM, tn), lambda i, j: (i, j))],
        out_specs=pl.BlockSpec((TM, tn), lambda i, j: (i, j)),
        out_shape=jax.ShapeDtypeStruct((m, n), F32),
        compiler_params=_cparams(("parallel", "arbitrary")),
        name="matmul_residual",
    )(a, w, res)


def _mix_mm_res_kernel(a_ref, y_ref, z_ref, u_ref, d_ref, ga_ref, gy_ref, w_ref, r_ref, o_ref, mix_ref):
    @pl.when(pl.program_id(1) == 0)
    def _():
        a = a_ref[...]
        mix_ref[:, :ATT_WIDTH] = (a * _rms(a) * ga_ref[...]).astype(BF16)
        cat = lambda ref: jnp.concatenate([ref[c] for c in range(NCB)], axis=-1)
        y = (cat(y_ref) + cat(z_ref) * d_ref[...]) * cat(u_ref)
        mix_ref[:, ATT_WIDTH:] = (y * _rms(y) * gy_ref[...]).astype(BF16)

    o_ref[...] = r_ref[...] + _dot(mix_ref[...], w_ref[...])


def mix_matmul_residual(a, y, z, u, d, ga, gy, w, res, tn):
    m = a.shape[0]
    n = w.shape[1]
    blk = pl.BlockSpec((NCB, TM, LANE), lambda i, j: (0, i, 0))
    row = lambda width: pl.BlockSpec((1, width), lambda i, j: (0, 0))
    return pl.pallas_call(
        _mix_mm_res_kernel,
        grid=(m // TM, n // tn),
        in_specs=[pl.BlockSpec((TM, ATT_WIDTH), lambda i, j: (i, 0)), blk, blk, blk,
                  row(HY_WIDTH), row(ATT_WIDTH), row(HY_WIDTH),
                  pl.BlockSpec((ATT_WIDTH + HY_WIDTH, tn), lambda i, j: (0, j)),
                  pl.BlockSpec((TM, tn), lambda i, j: (i, j))],
        out_specs=pl.BlockSpec((TM, tn), lambda i, j: (i, j)),
        out_shape=jax.ShapeDtypeStruct((m, n), F32),
        scratch_shapes=[pltpu.VMEM((TM, ATT_WIDTH + HY_WIDTH), BF16)],
        compiler_params=_cparams(("parallel", "arbitrary")),
        name="mix_matmul_residual",
    )(a, y, z, u, d, ga, gy, w, res)


def _mla_prep_kernel(p_ref, cos_ref, sin_ref, glq_ref, glkv_ref, wq_ref, wkv_ref, gq_ref, gk_ref,
                     q_ref, k_ref, v_ref):
    p = p_ref[...]
    cq = p[:, :Q_LORA]
    ckv = p[:, Q_LORA:Q_LORA + KV_LORA]
    kr = p[:, Q_LORA + KV_LORA:Q_LORA + KV_LORA + LANE]
    qp = _dot((cq * _rms(cq) * glq_ref[...]).astype(BF16), wq_ref[...])
    kvp = _dot((ckv * _rms(ckv) * glkv_ref[...]).astype(BF16), wkv_ref[...])
    cosp = cos_ref[...]
    sinp = sin_ref[...]
    rope_lane = lax.broadcasted_iota(jnp.int32, (1, LANE), 1) < QK_ROPE
    gq = gq_ref[...]
    gk = gk_ref[...]
    q_cos, q_sin = cosp * gq[1:2], sinp * gq[2:3]
    k_cos, k_sin = cosp * gk[1:2], sinp * gk[2:3]
    kr_ss = jnp.sum(jnp.where(rope_lane, kr * kr, 0.0), axis=-1, keepdims=True)
    kr_rot = kr * k_cos + pltpu.roll(kr, QK_ROPE, 1) * k_sin
    q_scale = QK_DIM ** -0.5 * math.log2(math.e)
    for h in range(ATT_HEADS):
        qn = qp[:, h * HEAD_PAD:h * HEAD_PAD + LANE]
        qa = qp[:, h * HEAD_PAD + LANE:(h + 1) * HEAD_PAD]
        ss = (jnp.sum(qn * qn, axis=-1, keepdims=True)
              + jnp.sum(jnp.where(rope_lane, qa * qa, 0.0), axis=-1, keepdims=True))
        rq = lax.rsqrt(ss * (1.0 / QK_DIM) + EPS) * q_scale
        q_ref[h, :, :LANE] = (qn * rq * gq[0:1]).astype(BF16)
        q_ref[h, :, LANE:] = ((qa * q_cos + pltpu.roll(qa, QK_ROPE, 1) * q_sin) * rq).astype(BF16)
        kn = kvp[:, h * HEAD_PAD:h * HEAD_PAD + LANE]
        rk = lax.rsqrt((jnp.sum(kn * kn, axis=-1, keepdims=True) + kr_ss) * (1.0 / QK_DIM) + EPS)
        k_ref[h, :, :LANE] = (kn * rk * gk[0:1]).astype(BF16)
        k_ref[h, :, LANE:] = (kr_rot * rk).astype(BF16)
        v_ref[h] = kvp[:, h * HEAD_PAD + LANE:(h + 1) * HEAD_PAD].astype(BF16)


def mla_prep(p, cosp, sinp, glq, glkv, wq, wkv, gq, gk):
    const = lambda i: (0, 0)
    return pl.pallas_call(
        _mla_prep_kernel,
        grid=(TP // TM,),
        in_specs=[pl.BlockSpec((TM, P_ATT), lambda i: (i, 0)),
                  pl.BlockSpec((TM, LANE), lambda i: (i, 0)),
                  pl.BlockSpec((TM, LANE), lambda i: (i, 0)),
                  pl.BlockSpec((1, Q_LORA), const),
                  pl.BlockSpec((1, KV_LORA), const),
                  pl.BlockSpec((Q_LORA, ATT_HEADS * HEAD_PAD), const),
                  pl.BlockSpec((KV_LORA, ATT_HEADS * HEAD_PAD), const),
                  pl.BlockSpec((3, LANE), const),
                  pl.BlockSpec((3, LANE), const)],
        out_specs=[pl.BlockSpec((ATT_HEADS, TM, HEAD_PAD), lambda i: (0, i, 0)),
                   pl.BlockSpec((ATT_HEADS, TM, HEAD_PAD), lambda i: (0, i, 0)),
                   pl.BlockSpec((ATT_HEADS, TM, V_DIM), lambda i: (0, i, 0))],
        out_shape=[jax.ShapeDtypeStruct((ATT_HEADS, TP, HEAD_PAD), BF16),
                   jax.ShapeDtypeStruct((ATT_HEADS, TP, HEAD_PAD), BF16),
                   jax.ShapeDtypeStruct((ATT_HEADS, TP, V_DIM), BF16)],
        compiler_params=_cparams(("parallel",)),
        name="mla_prep",
    )(p, cosp, sinp, glq, glkv, wq, wkv, gq, gk)


def _attn_kernel(q_ref, k_ref, v_ref, o_ref):
    q = q_ref[0]
    m = jnp.full((TQ, 1), -jnp.inf, F32)
    l = jnp.zeros((TQ, 1), F32)
    acc = jnp.zeros((TQ, V_DIM), F32)
    for kb in range(TP // TK):
        k = k_ref[0, kb * TK:(kb + 1) * TK, :]
        s = lax.dot_general(q, k, (((1,), (1,)), ((), ())), preferred_element_type=F32)
        if (kb + 1) * TK > T:
            col = kb * TK + lax.broadcasted_iota(jnp.int32, (1, TK), 1)
            s = jnp.where(col < T, s, -jnp.inf)
        m_new = jnp.maximum(m, jnp.max(s, axis=-1, keepdims=True))
        alpha = jnp.exp2(m - m_new)
        pr = jnp.exp2(s - m_new)
        l = alpha * l + jnp.sum(pr, axis=-1, keepdims=True)
        acc = alpha * acc + _dot(pr.astype(BF16), v_ref[0, kb * TK:(kb + 1) * TK, :])
        m = m_new
    o_ref[...] = acc / l


def attention(q, k, v):
    return pl.pallas_call(
        _attn_kernel,
        grid=(ATT_HEADS, TP // TQ),
        in_specs=[pl.BlockSpec((1, TQ, HEAD_PAD), lambda h, i: (h, i, 0)),
                  pl.BlockSpec((1, TP, HEAD_PAD), lambda h, i: (h, 0, 0)),
                  pl.BlockSpec((1, TP, V_DIM), lambda h, i: (h, 0, 0))],
        out_specs=pl.BlockSpec((TQ, V_DIM), lambda h, i: (i, h)),
        out_shape=jax.ShapeDtypeStruct((TP, ATT_WIDTH), F32),
        compiler_params=_cparams(("parallel", "parallel")),
        name="attention",
    )(q, k, v)


def _short_conv(x, prev_row, next_row, w, b, t0):
    rows = x.shape[0]
    r = lax.broadcasted_iota(jnp.int32, (rows, 1), 0)
    t = t0 + r
    xm = jnp.where(r == 0, prev_row, pltpu.roll(x, 1, 0))
    xm = jnp.where(t == 0, 0.0, xm)
    xp = jnp.where(r == rows - 1, next_row, pltpu.roll(x, rows - 1, 0))
    xp = jnp.where(t >= T - 1, 0.0, xp)
    return xm * w[0:1] + x * w[1:2] + xp * w[2:3] + b


def _hy_pre_kernel(x0_ref, x1_ref, v_ref, x0p_ref, x1p_ref, vp_ref, x0n_ref, x1n_ref, vn_ref,
                   w0_ref, w1_ref, wv_ref, b0_ref, b1_ref, bv_ref, z_ref, u_ref):
    t0 = pl.program_id(0) * TH
    valid = (t0 + lax.broadcasted_iota(jnp.int32, (TH, 1), 0)) < T
    u0 = _short_conv(x0_ref[...], x0p_ref[7:8], x0n_ref[0:1], w0_ref[...], b0_ref[...], t0)
    u1 = _short_conv(x1_ref[...], x1p_ref[7:8], x1n_ref[0:1], w1_ref[...], b1_ref[...], t0)
    uv = _short_conv(v_ref[...], vp_ref[7:8], vn_ref[0:1], wv_ref[...], bv_ref[...], t0)
    z = jnp.where(valid, uv * u1, 0.0)
    u = jnp.where(valid, u0, 0.0)
    for c in range(z_ref.shape[0]):
        z_ref[c] = z[:, c * LANE:(c + 1) * LANE]
        u_ref[c] = u[:, c * LANE:(c + 1) * LANE]


def hyena_pre(p, conv_w, conv_b):
    cw = 512
    ncw = HY_WIDTH // cw
    off = P_ATT // cw
    halo = TH // 8
    last8 = TP // 8 - 1

    def main(s):
        return pl.BlockSpec((TH, cw), lambda i, j: (i, off + s * ncw + j))

    def prev(s):
        return pl.BlockSpec((8, cw), lambda i, j: (jnp.maximum(i * halo - 1, 0), off + s * ncw + j))

    def nxt(s):
        return pl.BlockSpec((8, cw), lambda i, j: (jnp.minimum((i + 1) * halo, last8), off + s * ncw + j))

    def par(s, rows):
        return pl.BlockSpec((rows, cw), lambda i, j: (0, s * ncw + j))

    out_spec = pl.BlockSpec((cw // LANE, TH, LANE), lambda i, j: (j, i, 0))
    out_sds = jax.ShapeDtypeStruct((NCB, TP, LANE), F32)
    return pl.pallas_call(
        _hy_pre_kernel,
        grid=(pl.cdiv(TP, TH), ncw),
        in_specs=[main(0), main(1), main(2), prev(0), prev(1), prev(2), nxt(0), nxt(1), nxt(2),
                  par(0, 3), par(1, 3), par(2, 3), par(0, 1), par(1, 1), par(2, 1)],
        out_specs=[out_spec, out_spec],
        out_shape=[out_sds, out_sds],
        compiler_params=_cparams(("parallel", "parallel")),
        name="hyena_pre",
    )(p, p, p, p, p, p, p, p, p, conv_w, conv_w, conv_w, conv_b, conv_b, conv_b)


def _filter_kernel(zt_ref, aux_ref, w1_ref, b1_ref, f1_ref, w2_ref, b2_ref, f2_ref, w3_ref, dl_ref,
                   hf_ref, hb_ref):
    hi = lax.Precision.HIGHEST
    h = jnp.sin(f1_ref[...] * (jnp.dot(w1_ref[...], zt_ref[...], precision=hi,
                                       preferred_element_type=F32) + b1_ref[...]))
    h = jnp.sin(f2_ref[...] * (jnp.dot(w2_ref[...], h, precision=hi,
                                       preferred_element_type=F32) + b2_ref[...]))
    hh = lax.dot_general(h.astype(BF16), w3_ref[...], (((0,), (0,)), ((), ())),
                         preferred_element_type=F32)
    aux = aux_ref[...]
    decay = jnp.exp(-aux[:, 0:1] * dl_ref[...])
    hf = hh[:, :HY_WIDTH] * (decay * aux[:, 1:2])
    hb = hh[:, HY_WIDTH:] * (decay * aux[:, 2:3])
    for c in range(NCB):
        hf_ref[c] = hf[:, c * LANE:(c + 1) * LANE]
        hb_ref[c] = hb[:, c * LANE:(c + 1) * LANE]


def hyena_filter(zt, aux, w1t, b1, f1, w2t, b2, f2, w3, deltas):
    const = lambda i: (0, 0)
    out_spec = pl.BlockSpec((NCB, TM, LANE), lambda i: (0, i, 0))
    out_sds = jax.ShapeDtypeStruct((NCB, TP, LANE), F32)
    col = pl.BlockSpec((FILT_ORDER, 1), const)
    return pl.pallas_call(
        _filter_kernel,
        grid=(TP // TM,),
        in_specs=[pl.BlockSpec((FILT_ORDER, TM), lambda i: (0, i)),
                  pl.BlockSpec((TM, 8), lambda i: (i, 0)),
                  pl.BlockSpec((FILT_ORDER, FILT_ORDER), const), col, col,
                  pl.BlockSpec((FILT_ORDER, FILT_ORDER), const), col, col,
                  pl.BlockSpec((FILT_ORDER, 2 * HY_WIDTH), const),
                  pl.BlockSpec((1, HY_WIDTH), const)],
        out_specs=[out_spec, out_spec],
        out_shape=[out_sds, out_sds],
        compiler_params=_cparams(("parallel",)),
        name="hyena_filter",
    )(zt, aux, w1t, b1, f1, w2t, b2, f2, w3, deltas)


def _stage_a(x_ref, fa_ref, zs_ref, as_ref):
    for n1 in range(NB):
        zs_ref[n1 * PZ:n1 * PZ + N2, :] = x_ref[0, n1 * N2:(n1 + 1) * N2, :]
    zs_ref[NB * PZ:, :] = jnp.zeros(((NBP - NB) * PZ, LANE), F32)
    kpad = jnp.zeros((fa_ref.shape[1] - NBP, 2 * LANE), F32)

    def body(j, carry):
        n2 = 2 * j
        x = jnp.concatenate([zs_ref[pl.ds(n2, NBP, stride=PZ), :],
                             zs_ref[pl.ds(n2 + 1, NBP, stride=PZ), :]], axis=1)
        r = _dot(fa_ref[...], jnp.concatenate([x, kpad], axis=0).astype(BF16))
        off = pl.multiple_of(n2 * PA, 8)
        as_ref[pl.ds(off, PA), :] = r[:PA, :LANE]
        as_ref[pl.ds(off + PA, PA), :] = r[:PA, LANE:]
        return carry

    lax.fori_loop(0, N2 // 2, body, 0, unroll=4)


def _stage_b_fwd(as_ref, fb, k1):
    ar = as_ref[pl.ds(k1, N2, stride=PA), :]
    ai = as_ref[pl.ds(IM0 + k1, N2, stride=PA), :]
    return _dot(fb, jnp.concatenate([ar, ai], axis=0).astype(BF16))


def _filter_spectrum_kernel(hf_ref, hb_ref, fa_ref, fb_ref, o_ref, zs_ref, af_ref, ab_ref):
    kc = pl.program_id(1)

    @pl.when(kc == 0)
    def _():
        _stage_a(hf_ref, fa_ref, zs_ref, af_ref)
        _stage_a(hb_ref, fa_ref, zs_ref, ab_ref)

    for k in range(KC):
        k1 = kc * KC + k
        sf = _stage_b_fwd(af_ref, fb_ref[k], k1)
        sb = _stage_b_fwd(ab_ref, fb_ref[k], k1)
        o_ref[0, k, :N2, :] = sf[:N2] + sb[:N2]
        o_ref[0, k, N2:, :] = sf[N2:] - sb[N2:]


def filter_spectrum(hf, hb, fa, fb):
    blk = pl.BlockSpec((1, TP, LANE), lambda c, k: (c, 0, 0))
    return pl.pallas_call(
        _filter_spectrum_kernel,
        grid=(NCB, NKC),
        in_specs=[blk, blk,
                  pl.BlockSpec(fa.shape, lambda c, k: (0, 0)),
                  pl.BlockSpec((KC, 2 * N2, 2 * N2), lambda c, k: (k, 0, 0))],
        out_specs=pl.BlockSpec((1, KC, 2 * N2, LANE), lambda c, k: (c, k, 0, 0)),
        out_shape=jax.ShapeDtypeStruct((NCB, KH, 2 * N2, LANE), F32),
        scratch_shapes=[pltpu.VMEM((NBP * PZ, LANE), F32),
                        pltpu.VMEM((N2 * PA, LANE), F32),
                        pltpu.VMEM((N2 * PA, LANE), F32)],
        compiler_params=_cparams(("parallel", "arbitrary")),
        name="filter_spectrum",
    )(hf, hb, fa, fb)


def _long_conv_kernel(z_ref, kf_ref, fa_ref, fb_ref, gb_ref, ga_ref, y_ref, zs_ref, as_ref, bs_ref):
    kc = pl.program_id(1)

    @pl.when(kc == 0)
    def _():
        _stage_a(z_ref, fa_ref, zs_ref, as_ref)
        bs_ref[KH * PB:, :] = jnp.zeros(((NBP - KH) * PB, LANE), F32)

    for k in range(KC):
        k1 = kc * KC + k
        xs = _stage_b_fwd(as_ref, fb_ref[k], k1)
        xr, xi = xs[:N2], xs[N2:]
        fr, fi = kf_ref[0, k, :N2, :], kf_ref[0, k, N2:, :]
        ys = jnp.concatenate([xr * fr - xi * fi, xr * fi + xi * fr], axis=0).astype(BF16)
        bs_ref[pl.ds(pl.multiple_of(k1 * PB, 8), 2 * N2), :] = _dot(gb_ref[k], ys)

    @pl.when(kc == NKC - 1)
    def _():
        kpad = jnp.zeros((LANE - NBP, 2 * LANE), F32)

        def body(j, carry):
            n2 = 2 * j
            cr = jnp.concatenate([bs_ref[pl.ds(n2, NBP, stride=PB), :],
                                  bs_ref[pl.ds(n2 + 1, NBP, stride=PB), :]], axis=1)
            ci = jnp.concatenate([bs_ref[pl.ds(N2 + n2, NBP, stride=PB), :],
                                  bs_ref[pl.ds(N2 + n2 + 1, NBP, stride=PB), :]], axis=1)
            c = jnp.concatenate([cr, kpad, ci, kpad], axis=0).astype(BF16)
            y = _dot(ga_ref[...], c)
            zs_ref[pl.ds(n2, NBP, stride=PZ), :] = y[:, :LANE]
            zs_ref[pl.ds(n2 + 1, NBP, stride=PZ), :] = y[:, LANE:]
            return carry

        lax.fori_loop(0, N2 // 2, body, 0, unroll=4)
        for n1 in range(NB):
            y_ref[0, n1 * N2:(n1 + 1) * N2, :] = zs_ref[n1 * PZ:n1 * PZ + N2, :]


def long_conv(z, kf, fa, fb, gb, ga):
    blk = pl.BlockSpec((1, TP, LANE), lambda c, k: (c, 0, 0))
    mat = pl.BlockSpec((KC, 2 * N2, 2 * N2), lambda c, k: (k, 0, 0))
    return pl.pallas_call(
        _long_conv_kernel,
        grid=(NCB, NKC),
        in_specs=[blk,
                  pl.BlockSpec((1, KC, 2 * N2, LANE), lambda c, k: (c, k, 0, 0)),
                  pl.BlockSpec(fa.shape, lambda c, k: (0, 0)),
                  mat, mat,
                  pl.BlockSpec(ga.shape, lambda c, k: (0, 0))],
        out_specs=blk,
        out_shape=jax.ShapeDtypeStruct((NCB, TP, LANE), F32),
        scratch_shapes=[pltpu.VMEM((NBP * PZ, LANE), F32),
                        pltpu.VMEM((N2 * PA, LANE), F32),
                        pltpu.VMEM((NBP * PB, LANE), F32)],
        compiler_params=_cparams(("parallel", "arbitrary")),
        name="long_conv",
    )(z, kf, fa, fb, gb, ga)


@functools.lru_cache(maxsize=None)
def _dft_constants():
    k1 = np.arange(KH, dtype=np.int64)

    n1 = np.arange(NB, dtype=np.int64)
    ang = 2.0 * np.pi * ((k1[:, None] * n1[None, :]) % N1) / N1
    fa = np.zeros((PA + 8, LANE))
    fa[:KH, :NB] = np.cos(ang)
    fa[IM0:IM0 + KH, :NB] = -np.sin(ang)

    n2 = np.arange(N2, dtype=np.int64)
    k2 = np.arange(N2, dtype=np.int64)
    freq = k1[:, None, None] + N1 * k2[None, :, None]
    theta = 2.0 * np.pi * ((freq * n2[None, None, :]) % NFFT) / NFFT
    c, s = np.cos(theta), np.sin(theta)
    fb = np.concatenate([np.concatenate([c, s], axis=2), np.concatenate([-s, c], axis=2)], axis=1)
    gb = np.transpose(fb, (0, 2, 1))

    herm = np.where((k1 == 0) | (k1 == N1 // 2), 1.0, 2.0) / NFFT
    phi = 2.0 * np.pi * ((n1[:, None] * k1[None, :]) % N1) / N1
    ga = np.zeros((NBP, 2 * LANE))
    ga[:NB, :KH] = np.cos(phi) * herm[None, :]
    ga[:NB, LANE:LANE + KH] = -np.sin(phi) * herm[None, :]
    return tuple(m.astype(np.float32) for m in (fa, fb, gb, ga))


@functools.lru_cache(maxsize=None)
def _position_tables():
    pos = np.arange(TP, dtype=np.float64)
    inv = ROPE_THETA ** (-np.arange(0, QK_ROPE, 2, dtype=np.float64) / QK_ROPE)
    ang = pos[:, None] * inv[None, :]
    ang = np.concatenate([ang, ang], axis=-1)
    pad = np.zeros((TP, LANE - QK_ROPE))
    cosp = np.concatenate([np.cos(ang), pad], axis=-1)
    sinp = np.concatenate([np.sin(ang), pad], axis=-1)

    p = np.minimum(np.arange(TP), T - 1).astype(np.float64)
    tl = p / (T - 1)
    freqs = np.linspace(1e-4, FILT_BANDS - 1, FILT_BANDS)
    fang = (2.0 * np.pi * p / T)[None, :] * freqs[:, None]
    zt = np.zeros((FILT_ORDER, TP))
    zt[0] = tl
    zt[1:1 + FILT_BANDS] = np.cos(fang)
    zt[1 + FILT_BANDS:FILT_EMB] = -np.sin(fang)
    live = np.arange(TP) < T
    aux = np.zeros((TP, 8))
    aux[:, 0] = tl
    aux[:, 1] = live
    aux[:, 2] = live & (np.arange(TP) >= 1)
    max_decay = math.log(DECAY_TARGET) / FAST_DECAY_PCT
    min_decay = math.log(DECAY_TARGET) / SLOW_DECAY_PCT
    deltas = np.abs(np.linspace(min_decay, max_decay, HY_WIDTH))[None, :]
    return tuple(m.astype(np.float32) for m in (cosp, sinp, zt, aux, deltas))


def _rot_cols(w):
    half = QK_ROPE // 2
    return jnp.concatenate([-w[..., half:], w[..., :half]], axis=-1)


def _rope_gain_rows(g):
    half = QK_ROPE // 2
    gr = g[QK_NOPE:]
    pad = jnp.zeros((LANE - QK_ROPE,), F32)
    return jnp.stack([g[:QK_NOPE], jnp.concatenate([gr, pad]),
                      jnp.concatenate([gr[half:], gr[:half], pad])])


def kernel(x, meta_tokens, norm_mix_g, w_in, q_lat_g, kv_lat_g, w_uq, w_ukv, q_norm_g, k_norm_g, conv_w, conv_b, filt_w1, filt_b1, filt_freq1, filt_w2, filt_b2, filt_freq2, filt_w3, hy_skip, attn_out_g, hy_out_g, w_out, norm_ffn_g, w_gate, w_up, w_down):
    assert x.shape == (1, SEQ, D_MODEL)
    h = jnp.concatenate([meta_tokens.astype(F32), x[0], jnp.zeros((TP - T, D_MODEL), F32)], axis=0)

    cosp, sinp, zt, aux, deltas = (jnp.asarray(m) for m in _position_tables())
    fa, fb, gb, ga = (jnp.asarray(m).astype(BF16) for m in _dft_constants())
    epad = FILT_ORDER - FILT_EMB
    col = lambda v: v[:, None]

    o1, o2, o3 = Q_LORA, Q_LORA + KV_LORA, Q_LORA + KV_LORA + QK_ROPE
    for l in range(DEPTH):
        w_in_l = w_in[l].astype(BF16)
        w_in_big = jnp.concatenate(
            [w_in_l[:, :o3], _rot_cols(w_in_l[:, o2:o3]), jnp.zeros((D_MODEL, P_ATT - o3 - QK_ROPE), BF16),
             w_in_l[:, o3:]], axis=1)
        wq = w_uq[l].astype(BF16).reshape(Q_LORA, ATT_HEADS, QK_DIM)
        wq = jnp.concatenate([wq, _rot_cols(wq[..., QK_NOPE:])], axis=-1)
        wq = wq.reshape(Q_LORA, ATT_HEADS * HEAD_PAD)
        wkv = w_ukv[l].astype(BF16)

        p = norm_matmul(h, norm_mix_g[l][None], w_in_big, 1024)
        q, k, v = mla_prep(p, cosp, sinp, q_lat_g[l][None], kv_lat_g[l][None], wq, wkv,
                           _rope_gain_rows(q_norm_g[l]), _rope_gain_rows(k_norm_g[l]))
        att = attention(q, k, v)

        z, u = hyena_pre(p, conv_w[l], conv_b[l][None])
        hf, hb = hyena_filter(
            zt, aux,
            jnp.pad(filt_w1[l].T, ((0, 0), (0, epad))), col(filt_b1[l]), col(filt_freq1[l]),
            filt_w2[l].T, col(filt_b2[l]), col(filt_freq2[l]),
            filt_w3[l].astype(BF16), deltas)
        kf = filter_spectrum(hf, hb, fa, fb)
        y = long_conv(z, kf, fa, fb, gb, ga)

        h = mix_matmul_residual(att, y, z, u, hy_skip[l][None], attn_out_g[l][None], hy_out_g[l][None],
                                w_out[l].astype(BF16), h, 1024)

        act = norm_swiglu(h, norm_ffn_g[l][None], w_gate[l].astype(BF16), w_up[l].astype(BF16), 512)
        h = matmul_residual(act, w_down[l].astype(BF16), h, 1024)

    return h[N_META:T][None]
```

```python
import functools
import math

import numpy as np
import jax
import jax.numpy as jnp
from jax import lax
from jax.experimental import pallas as pl
from jax.experimental.pallas import tpu as pltpu

F32 = jnp.float32
BF16 = jnp.bfloat16

D_MODEL = 2048
SEQ = 8192
DEPTH = 4
N_META = 16
T = N_META + SEQ
V_DIM = 128
QK_NOPE = 128
QK_ROPE = 64
QK_DIM = QK_NOPE + QK_ROPE
ATT_HEADS = 8
ATT_WIDTH = ATT_HEADS * V_DIM
Q_LORA = 512
KV_LORA = 256
HY_WIDTH = 1024
FILT_EMB = 33
FILT_BANDS = 16
FILT_ORDER = 64
DECAY_TARGET = 1e-2
FAST_DECAY_PCT = 0.3
SLOW_DECAY_PCT = 1.5
D_FF = 5632
ROPE_THETA = 10000.0
EPS = 1e-6

LANE = 128
HEAD_PAD = 2 * LANE
VMEM_LIMIT = 56 * 1024 * 1024

TP = 8320
TM = 640
TQ = 1040
TK = 1792
P_COLS = 4096
P_ATT = 1024
TH = 1024

N2 = 128
N1 = 130
NFFT = N1 * N2
KH = N1 // 2 + 1
KC = 11
NKC = KH // KC
NB = TP // N2
NBP = 72
NCB = HY_WIDTH // LANE
PZ = 136
PA = 136
IM0 = 68
PB = 264


def _cparams(sem):
    return pltpu.CompilerParams(dimension_semantics=sem, vmem_limit_bytes=VMEM_LIMIT)


def _rms(x):
    return lax.rsqrt(jnp.mean(x * x, axis=-1, keepdims=True) + EPS)


def _dot(a, b):
    return jnp.dot(a, b, preferred_element_type=F32)


def _norm_mm_kernel(x_ref, g_ref, w_ref, o_ref, xn_ref):
    @pl.when(pl.program_id(1) == 0)
    def _():
        x = x_ref[...]
        xn_ref[...] = (x * _rms(x) * g_ref[...]).astype(BF16)

    o_ref[...] = _dot(xn_ref[...], w_ref[...])


def _layer_spec(l, k, tn):
    return pl.BlockSpec((None, k, tn), lambda i, j: (l, 0, j))


def norm_matmul(x, g, w, l, tn):
    m, k = x.shape
    n = w.shape[2]
    return pl.pallas_call(
        _norm_mm_kernel,
        grid=(m // TM, n // tn),
        in_specs=[pl.BlockSpec((TM, k), lambda i, j: (i, 0)),
                  pl.BlockSpec((1, k), lambda i, j: (0, 0)),
                  _layer_spec(l, k, tn)],
        out_specs=pl.BlockSpec((TM, tn), lambda i, j: (i, j)),
        out_shape=jax.ShapeDtypeStruct((m, n), F32),
        scratch_shapes=[pltpu.VMEM((TM, k), BF16)],
        compiler_params=_cparams(("parallel", "arbitrary")),
        name="norm_matmul",
    )(x, g, w)


def _norm_swiglu_kernel(x_ref, g_ref, wg_ref, wu_ref, o_ref, xn_ref):
    @pl.when(pl.program_id(1) == 0)
    def _():
        x = x_ref[...]
        xn_ref[...] = (x * _rms(x) * g_ref[...]).astype(BF16)

    xn = xn_ref[...]
    a = _dot(xn, wg_ref[...])
    b = _dot(xn, wu_ref[...])
    o_ref[...] = (a * jax.nn.sigmoid(a) * b).astype(BF16)


def norm_swiglu(x, g, wg, wu, l, tn):
    m, k = x.shape
    n = wg.shape[2]
    return pl.pallas_call(
        _norm_swiglu_kernel,
        grid=(m // TM, n // tn),
        in_specs=[pl.BlockSpec((TM, k), lambda i, j: (i, 0)),
                  pl.BlockSpec((1, k), lambda i, j: (0, 0)),
                  _layer_spec(l, k, tn),
                  _layer_spec(l, k, tn)],
        out_specs=pl.BlockSpec((TM, tn), lambda i, j: (i, j)),
        out_shape=jax.ShapeDtypeStruct((m, n), BF16),
        scratch_shapes=[pltpu.VMEM((TM, k), BF16)],
        compiler_params=_cparams(("parallel", "arbitrary")),
        name="norm_swiglu",
    )(x, g, wg, wu)


def _mm_res_kernel(a_ref, w_ref, r_ref, o_ref):
    o_ref[...] = r_ref[...] + _dot(a_ref[...], w_ref[...])


def matmul_residual(a, w, l, res, tn):
    m, k = a.shape
    n = w.shape[2]
    return pl.pallas_call(
        _mm_res_kernel,
        grid=(m // TM, n // tn),
        in_specs=[pl.BlockSpec((TM, k), lambda i, j: (i, 0)),
                  _layer_spec(l, k, tn),
                  pl.BlockSpec((TM, tn), lambda i, j: (i, j))],
        out_specs=pl.BlockSpec((TM, tn), lambda i, j: (i, j)),
        out_shape=jax.ShapeDtypeStruct((m, n), F32),
        compiler_params=_cparams(("parallel", "arbitrary")),
        name="matmul_residual",
    )(a, w, res)


def _mix_mm_res_kernel(a_ref, y_ref, z_ref, u_ref, d_ref, ga_ref, gy_ref, w_ref, r_ref, o_ref, mix_ref):
    @pl.when(pl.program_id(1) == 0)
    def _():
        a = a_ref[...]
        mix_ref[:, :ATT_WIDTH] = (a * _rms(a) * ga_ref[...]).astype(BF16)
        cat = lambda ref: jnp.concatenate([ref[c] for c in range(NCB)], axis=-1)
        y = (cat(y_ref) + cat(z_ref) * d_ref[...]) * cat(u_ref)
        mix_ref[:, ATT_WIDTH:] = (y * _rms(y) * gy_ref[...]).astype(BF16)

    o_ref[...] = r_ref[...] + _dot(mix_ref[...], w_ref[...])


def mix_matmul_residual(a, y, z, u, d, ga, gy, w, l, res, tn):
    m = a.shape[0]
    n = w.shape[2]
    blk = pl.BlockSpec((NCB, TM, LANE), lambda i, j: (0, i, 0))
    row = lambda width: pl.BlockSpec((1, width), lambda i, j: (0, 0))
    return pl.pallas_call(
        _mix_mm_res_kernel,
        grid=(m // TM, n // tn),
        in_specs=[pl.BlockSpec((TM, ATT_WIDTH), lambda i, j: (i, 0)), blk, blk, blk,
                  row(HY_WIDTH), row(ATT_WIDTH), row(HY_WIDTH),
                  _layer_spec(l, ATT_WIDTH + HY_WIDTH, tn),
                  pl.BlockSpec((TM, tn), lambda i, j: (i, j))],
        out_specs=pl.BlockSpec((TM, tn), lambda i, j: (i, j)),
        out_shape=jax.ShapeDtypeStruct((m, n), F32),
        scratch_shapes=[pltpu.VMEM((TM, ATT_WIDTH + HY_WIDTH), BF16)],
        compiler_params=_cparams(("parallel", "arbitrary")),
        name="mix_matmul_residual",
    )(a, y, z, u, d, ga, gy, w, res)


def _mla_prep_kernel(p_ref, cos_ref, sin_ref, glq_ref, glkv_ref, wq_ref, wkv_ref, gq_ref, gk_ref,
                     q_ref, k_ref, v_ref):
    p = p_ref[...]
    cq = p[:, :Q_LORA]
    ckv = p[:, Q_LORA:Q_LORA + KV_LORA]
    kr = p[:, Q_LORA + KV_LORA:Q_LORA + KV_LORA + LANE]
    qp = _dot((cq * _rms(cq) * glq_ref[...]).astype(BF16), wq_ref[...])
    kvp = _dot((ckv * _rms(ckv) * glkv_ref[...]).astype(BF16), wkv_ref[...])
    cosp = cos_ref[...]
    sinp = sin_ref[...]
    rope_lane = lax.broadcasted_iota(jnp.int32, (1, LANE), 1) < QK_ROPE
    gq = gq_ref[...]
    gk = gk_ref[...]
    q_cos, q_sin = cosp * gq[1:2], sinp * gq[2:3]
    k_cos, k_sin = cosp * gk[1:2], sinp * gk[2:3]
    kr_ss = jnp.sum(jnp.where(rope_lane, kr * kr, 0.0), axis=-1, keepdims=True)
    kr_rot = kr * k_cos + pltpu.roll(kr, QK_ROPE, 1) * k_sin
    q_scale = QK_DIM ** -0.5 * math.log2(math.e)
    for h in range(ATT_HEADS):
        qn = qp[:, h * HEAD_PAD:h * HEAD_PAD + LANE]
        qa = qp[:, h * HEAD_PAD + LANE:(h + 1) * HEAD_PAD]
        ss = (jnp.sum(qn * qn, axis=-1, keepdims=True)
              + jnp.sum(jnp.where(rope_lane, qa * qa, 0.0), axis=-1, keepdims=True))
        rq = lax.rsqrt(ss * (1.0 / QK_DIM) + EPS) * q_scale
        q_ref[h, :, :LANE] = (qn * rq * gq[0:1]).astype(BF16)
        q_ref[h, :, LANE:] = ((qa * q_cos + pltpu.roll(qa, QK_ROPE, 1) * q_sin) * rq).astype(BF16)
        kn = kvp[:, h * HEAD_PAD:h * HEAD_PAD + LANE]
        rk = lax.rsqrt((jnp.sum(kn * kn, axis=-1, keepdims=True) + kr_ss) * (1.0 / QK_DIM) + EPS)
        k_ref[h, :, :LANE] = (kn * rk * gk[0:1]).astype(BF16)
        k_ref[h, :, LANE:] = (kr_rot * rk).astype(BF16)
        v_ref[h] = kvp[:, h * HEAD_PAD + LANE:(h + 1) * HEAD_PAD].astype(BF16)


def mla_prep(p, cosp, sinp, glq, glkv, wq, wkv, l, gq, gk):
    const = lambda i: (0, 0)
    return pl.pallas_call(
        _mla_prep_kernel,
        grid=(TP // TM,),
        in_specs=[pl.BlockSpec((TM, P_ATT), lambda i: (i, 0)),
                  pl.BlockSpec((TM, LANE), lambda i: (i, 0)),
                  pl.BlockSpec((TM, LANE), lambda i: (i, 0)),
                  pl.BlockSpec((1, Q_LORA), const),
                  pl.BlockSpec((1, KV_LORA), const),
                  pl.BlockSpec((None, Q_LORA, ATT_HEADS * HEAD_PAD), lambda i: (l, 0, 0)),
                  pl.BlockSpec((None, KV_LORA, ATT_HEADS * HEAD_PAD), lambda i: (l, 0, 0)),
                  pl.BlockSpec((3, LANE), const),
                  pl.BlockSpec((3, LANE), const)],
        out_specs=[pl.BlockSpec((ATT_HEADS, TM, HEAD_PAD), lambda i: (0, i, 0)),
                   pl.BlockSpec((ATT_HEADS, TM, HEAD_PAD), lambda i: (0, i, 0)),
                   pl.BlockSpec((ATT_HEADS, TM, V_DIM), lambda i: (0, i, 0))],
        out_shape=[jax.ShapeDtypeStruct((ATT_HEADS, TP, HEAD_PAD), BF16),
                   jax.ShapeDtypeStruct((ATT_HEADS, TP, HEAD_PAD), BF16),
                   jax.ShapeDtypeStruct((ATT_HEADS, TP, V_DIM), BF16)],
        compiler_params=_cparams(("parallel",)),
        name="mla_prep",
    )(p, cosp, sinp, glq, glkv, wq, wkv, gq, gk)


def _attn_kernel(q_ref, k_ref, v_ref, o_ref):
    q = q_ref[0]
    m = jnp.full((TQ, 1), -jnp.inf, F32)
    l = jnp.zeros((TQ, 1), F32)
    acc = jnp.zeros((TQ, V_DIM), F32)
    for k0 in range(0, TP, TK):
        k1 = min(k0 + TK, TP)
        s = lax.dot_general(q, k_ref[0, k0:k1, :], (((1,), (1,)), ((), ())), preferred_element_type=F32)
        if k1 > T:
            col = k0 + lax.broadcasted_iota(jnp.int32, (1, k1 - k0), 1)
            s = jnp.where(col < T, s, -jnp.inf)
        m_new = jnp.maximum(m, jnp.max(s, axis=-1, keepdims=True))
        alpha = jnp.exp2(m - m_new)
        pr = jnp.exp2(s - m_new)
        l = alpha * l + jnp.sum(pr, axis=-1, keepdims=True)
        acc = alpha * acc + _dot(pr.astype(BF16), v_ref[0, k0:k1, :])
        m = m_new
    o_ref[...] = acc / l


def attention(q, k, v):
    return pl.pallas_call(
        _attn_kernel,
        grid=(ATT_HEADS, TP // TQ),
        in_specs=[pl.BlockSpec((1, TQ, HEAD_PAD), lambda h, i: (h, i, 0)),
                  pl.BlockSpec((1, TP, HEAD_PAD), lambda h, i: (h, 0, 0)),
                  pl.BlockSpec((1, TP, V_DIM), lambda h, i: (h, 0, 0))],
        out_specs=pl.BlockSpec((TQ, V_DIM), lambda h, i: (i, h)),
        out_shape=jax.ShapeDtypeStruct((TP, ATT_WIDTH), F32),
        compiler_params=_cparams(("parallel", "parallel")),
        name="attention",
    )(q, k, v)


def _short_conv(x, prev_row, next_row, w, b, t0):
    rows = x.shape[0]
    r = lax.broadcasted_iota(jnp.int32, (rows, 1), 0)
    t = t0 + r
    xm = jnp.where(r == 0, prev_row, pltpu.roll(x, 1, 0))
    xm = jnp.where(t == 0, 0.0, xm)
    xp = jnp.where(r == rows - 1, next_row, pltpu.roll(x, rows - 1, 0))
    xp = jnp.where(t >= T - 1, 0.0, xp)
    return xm * w[0:1] + x * w[1:2] + xp * w[2:3] + b


def _hy_pre_kernel(x0_ref, x1_ref, v_ref, x0p_ref, x1p_ref, vp_ref, x0n_ref, x1n_ref, vn_ref,
                   w0_ref, w1_ref, wv_ref, b0_ref, b1_ref, bv_ref, z_ref, u_ref):
    t0 = pl.program_id(0) * TH
    valid = (t0 + lax.broadcasted_iota(jnp.int32, (TH, 1), 0)) < T
    u0 = _short_conv(x0_ref[...], x0p_ref[7:8], x0n_ref[0:1], w0_ref[...], b0_ref[...], t0)
    u1 = _short_conv(x1_ref[...], x1p_ref[7:8], x1n_ref[0:1], w1_ref[...], b1_ref[...], t0)
    uv = _short_conv(v_ref[...], vp_ref[7:8], vn_ref[0:1], wv_ref[...], bv_ref[...], t0)
    z = jnp.where(valid, uv * u1, 0.0)
    u = jnp.where(valid, u0, 0.0)
    for c in range(z_ref.shape[0]):
        z_ref[c] = z[:, c * LANE:(c + 1) * LANE]
        u_ref[c] = u[:, c * LANE:(c + 1) * LANE]


def hyena_pre(p, conv_w, conv_b):
    cw = 512
    ncw = HY_WIDTH // cw
    off = P_ATT // cw
    halo = TH // 8
    last8 = TP // 8 - 1

    def main(s):
        return pl.BlockSpec((TH, cw), lambda i, j: (i, off + s * ncw + j))

    def prev(s):
        return pl.BlockSpec((8, cw), lambda i, j: (jnp.maximum(i * halo - 1, 0), off + s * ncw + j))

    def nxt(s):
        return pl.BlockSpec((8, cw), lambda i, j: (jnp.minimum((i + 1) * halo, last8), off + s * ncw + j))

    def par(s, rows):
        return pl.BlockSpec((rows, cw), lambda i, j: (0, s * ncw + j))

    out_spec = pl.BlockSpec((cw // LANE, TH, LANE), lambda i, j: (j, i, 0))
    out_sds = jax.ShapeDtypeStruct((NCB, TP, LANE), F32)
    return pl.pallas_call(
        _hy_pre_kernel,
        grid=(pl.cdiv(TP, TH), ncw),
        in_specs=[main(0), main(1), main(2), prev(0), prev(1), prev(2), nxt(0), nxt(1), nxt(2),
                  par(0, 3), par(1, 3), par(2, 3), par(0, 1), par(1, 1), par(2, 1)],
        out_specs=[out_spec, out_spec],
        out_shape=[out_sds, out_sds],
        compiler_params=_cparams(("parallel", "parallel")),
        name="hyena_pre",
    )(p, p, p, p, p, p, p, p, p, conv_w, conv_w, conv_w, conv_b, conv_b, conv_b)


def _filter_kernel(zt_ref, aux_ref, w1_ref, b1_ref, f1_ref, w2_ref, b2_ref, f2_ref, w3_ref, dl_ref,
                   hf_ref, hb_ref):
    hi = lax.Precision.HIGHEST
    h = jnp.sin(f1_ref[...] * (jnp.dot(w1_ref[...], zt_ref[...], precision=hi,
                                       preferred_element_type=F32) + b1_ref[...]))
    h = jnp.sin(f2_ref[...] * (jnp.dot(w2_ref[...], h, precision=hi,
                                       preferred_element_type=F32) + b2_ref[...]))
    hh = lax.dot_general(h.astype(BF16), w3_ref[...], (((0,), (0,)), ((), ())),
                         preferred_element_type=F32)
    aux = aux_ref[...]
    decay = jnp.exp(-aux[:, 0:1] * dl_ref[...])
    hf = hh[:, :HY_WIDTH] * (decay * aux[:, 1:2])
    hb = hh[:, HY_WIDTH:] * (decay * aux[:, 2:3])
    for c in range(NCB):
        hf_ref[c] = hf[:, c * LANE:(c + 1) * LANE]
        hb_ref[c] = hb[:, c * LANE:(c + 1) * LANE]


def hyena_filter(zt, aux, w1t, b1, f1, w2t, b2, f2, w3, deltas):
    const = lambda i: (0, 0)
    out_spec = pl.BlockSpec((NCB, TM, LANE), lambda i: (0, i, 0))
    out_sds = jax.ShapeDtypeStruct((NCB, TP, LANE), F32)
    col = pl.BlockSpec((FILT_ORDER, 1), const)
    return pl.pallas_call(
        _filter_kernel,
        grid=(TP // TM,),
        in_specs=[pl.BlockSpec((FILT_ORDER, TM), lambda i: (0, i)),
                  pl.BlockSpec((TM, 8), lambda i: (i, 0)),
                  pl.BlockSpec((FILT_ORDER, FILT_ORDER), const), col, col,
                  pl.BlockSpec((FILT_ORDER, FILT_ORDER), const), col, col,
                  pl.BlockSpec((FILT_ORDER, 2 * HY_WIDTH), const),
                  pl.BlockSpec((1, HY_WIDTH), const)],
        out_specs=[out_spec, out_spec],
        out_shape=[out_sds, out_sds],
        compiler_params=_cparams(("parallel",)),
        name="hyena_filter",
    )(zt, aux, w1t, b1, f1, w2t, b2, f2, w3, deltas)


def _stage_a(x_ref, fa_ref, zs_ref, as_ref):
    for n1 in range(NB):
        zs_ref[n1 * PZ:n1 * PZ + N2, :] = x_ref[0, n1 * N2:(n1 + 1) * N2, :]
    zs_ref[NB * PZ:, :] = jnp.zeros(((NBP - NB) * PZ, LANE), F32)
    kpad = jnp.zeros((fa_ref.shape[1] - NBP, 2 * LANE), F32)

    def body(j, carry):
        n2 = 2 * j
        x = jnp.concatenate([zs_ref[pl.ds(n2, NBP, stride=PZ), :],
                             zs_ref[pl.ds(n2 + 1, NBP, stride=PZ), :]], axis=1)
        r = _dot(fa_ref[...], jnp.concatenate([x, kpad], axis=0).astype(BF16))
        off = pl.multiple_of(n2 * PA, 8)
        as_ref[pl.ds(off, PA), :] = r[:PA, :LANE]
        as_ref[pl.ds(off + PA, PA), :] = r[:PA, LANE:]
        return carry

    lax.fori_loop(0, N2 // 2, body, 0, unroll=4)


def _stage_b_fwd(as_ref, fb, k1):
    ar = as_ref[pl.ds(k1, N2, stride=PA), :]
    ai = as_ref[pl.ds(IM0 + k1, N2, stride=PA), :]
    return _dot(fb, jnp.concatenate([ar, ai], axis=0).astype(BF16))


def _filter_spectrum_kernel(hf_ref, hb_ref, fa_ref, fb_ref, o_ref, zs_ref, af_ref, ab_ref):
    kc = pl.program_id(1)

    @pl.when(kc == 0)
    def _():
        _stage_a(hf_ref, fa_ref, zs_ref, af_ref)
        _stage_a(hb_ref, fa_ref, zs_ref, ab_ref)

    for k in range(KC):
        k1 = kc * KC + k
        sf = _stage_b_fwd(af_ref, fb_ref[k], k1)
        sb = _stage_b_fwd(ab_ref, fb_ref[k], k1)
        o_ref[0, k, :N2, :] = sf[:N2] + sb[:N2]
        o_ref[0, k, N2:, :] = sf[N2:] - sb[N2:]


def filter_spectrum(hf, hb, fa, fb):
    blk = pl.BlockSpec((1, TP, LANE), lambda c, k: (c, 0, 0))
    return pl.pallas_call(
        _filter_spectrum_kernel,
        grid=(NCB, NKC),
        in_specs=[blk, blk,
                  pl.BlockSpec(fa.shape, lambda c, k: (0, 0)),
                  pl.BlockSpec((KC, 2 * N2, 2 * N2), lambda c, k: (k, 0, 0))],
        out_specs=pl.BlockSpec((1, KC, 2 * N2, LANE), lambda c, k: (c, k, 0, 0)),
        out_shape=jax.ShapeDtypeStruct((NCB, KH, 2 * N2, LANE), F32),
        scratch_shapes=[pltpu.VMEM((NBP * PZ, LANE), F32),
                        pltpu.VMEM((N2 * PA, LANE), F32),
                        pltpu.VMEM((N2 * PA, LANE), F32)],
        compiler_params=_cparams(("parallel", "arbitrary")),
        name="filter_spectrum",
    )(hf, hb, fa, fb)


def _long_conv_kernel(z_ref, kf_ref, fa_ref, fb_ref, ga_ref, y_ref, zs_ref, as_ref, bs_ref):
    kc = pl.program_id(1)

    @pl.when(kc == 0)
    def _():
        _stage_a(z_ref, fa_ref, zs_ref, as_ref)
        bs_ref[KH * PB:, :] = jnp.zeros(((NBP - KH) * PB, LANE), F32)

    for k in range(KC):
        k1 = kc * KC + k
        xs = _stage_b_fwd(as_ref, fb_ref[k], k1)
        xr, xi = xs[:N2], xs[N2:]
        fr, fi = kf_ref[0, k, :N2, :], kf_ref[0, k, N2:, :]
        ys = jnp.concatenate([xr * fr - xi * fi, xr * fi + xi * fr], axis=0).astype(BF16)
        bs_ref[pl.ds(pl.multiple_of(k1 * PB, 8), 2 * N2), :] = lax.dot_general(
            fb_ref[k], ys, (((0,), (0,)), ((), ())), preferred_element_type=F32)

    @pl.when(kc == NKC - 1)
    def _():
        kpad = jnp.zeros((LANE - NBP, 2 * LANE), F32)

        def body(j, carry):
            n2 = 2 * j
            cr = jnp.concatenate([bs_ref[pl.ds(n2, NBP, stride=PB), :],
                                  bs_ref[pl.ds(n2 + 1, NBP, stride=PB), :]], axis=1)
            ci = jnp.concatenate([bs_ref[pl.ds(N2 + n2, NBP, stride=PB), :],
                                  bs_ref[pl.ds(N2 + n2 + 1, NBP, stride=PB), :]], axis=1)
            c = jnp.concatenate([cr, kpad, ci, kpad], axis=0).astype(BF16)
            y = _dot(ga_ref[...], c)
            zs_ref[pl.ds(n2, NBP, stride=PZ), :] = y[:, :LANE]
            zs_ref[pl.ds(n2 + 1, NBP, stride=PZ), :] = y[:, LANE:]
            return carry

        lax.fori_loop(0, N2 // 2, body, 0, unroll=4)
        for n1 in range(NB):
            y_ref[0, n1 * N2:(n1 + 1) * N2, :] = zs_ref[n1 * PZ:n1 * PZ + N2, :]


def long_conv(z, kf, fa, fb, ga):
    blk = pl.BlockSpec((1, TP, LANE), lambda c, k: (c, 0, 0))
    return pl.pallas_call(
        _long_conv_kernel,
        grid=(NCB, NKC),
        in_specs=[blk,
                  pl.BlockSpec((1, KC, 2 * N2, LANE), lambda c, k: (c, k, 0, 0)),
                  pl.BlockSpec(fa.shape, lambda c, k: (0, 0)),
                  pl.BlockSpec((KC, 2 * N2, 2 * N2), lambda c, k: (k, 0, 0)),
                  pl.BlockSpec(ga.shape, lambda c, k: (0, 0))],
        out_specs=blk,
        out_shape=jax.ShapeDtypeStruct((NCB, TP, LANE), F32),
        scratch_shapes=[pltpu.VMEM((NBP * PZ, LANE), F32),
                        pltpu.VMEM((N2 * PA, LANE), F32),
                        pltpu.VMEM((NBP * PB, LANE), F32)],
        compiler_params=_cparams(("parallel", "arbitrary")),
        name="long_conv",
    )(z, kf, fa, fb, ga)


@functools.lru_cache(maxsize=None)
def _dft_constants():
    k1 = np.arange(KH, dtype=np.int64)

    n1 = np.arange(NB, dtype=np.int64)
    ang = 2.0 * np.pi * ((k1[:, None] * n1[None, :]) % N1) / N1
    fa = np.zeros((PA + 8, LANE))
    fa[:KH, :NB] = np.cos(ang)
    fa[IM0:IM0 + KH, :NB] = -np.sin(ang)

    n2 = np.arange(N2, dtype=np.int64)
    k2 = np.arange(N2, dtype=np.int64)
    freq = k1[:, None, None] + N1 * k2[None, :, None]
    theta = 2.0 * np.pi * ((freq * n2[None, None, :]) % NFFT) / NFFT
    c, s = np.cos(theta), np.sin(theta)
    fb = np.concatenate([np.concatenate([c, s], axis=2), np.concatenate([-s, c], axis=2)], axis=1)

    herm = np.where((k1 == 0) | (k1 == N1 // 2), 1.0, 2.0) / NFFT
    phi = 2.0 * np.pi * ((n1[:, None] * k1[None, :]) % N1) / N1
    ga = np.zeros((NBP, 2 * LANE))
    ga[:NB, :KH] = np.cos(phi) * herm[None, :]
    ga[:NB, LANE:LANE + KH] = -np.sin(phi) * herm[None, :]
    return tuple(m.astype(np.float32) for m in (fa, fb, ga))


@functools.lru_cache(maxsize=None)
def _position_tables():
    pos = np.arange(TP, dtype=np.float64)
    inv = ROPE_THETA ** (-np.arange(0, QK_ROPE, 2, dtype=np.float64) / QK_ROPE)
    ang = pos[:, None] * inv[None, :]
    ang = np.concatenate([ang, ang], axis=-1)
    pad = np.zeros((TP, LANE - QK_ROPE))
    cosp = np.concatenate([np.cos(ang), pad], axis=-1)
    sinp = np.concatenate([np.sin(ang), pad], axis=-1)

    p = np.minimum(np.arange(TP), T - 1).astype(np.float64)
    tl = p / (T - 1)
    freqs = np.linspace(1e-4, FILT_BANDS - 1, FILT_BANDS)
    fang = (2.0 * np.pi * p / T)[None, :] * freqs[:, None]
    zt = np.zeros((FILT_ORDER, TP))
    zt[0] = tl
    zt[1:1 + FILT_BANDS] = np.cos(fang)
    zt[1 + FILT_BANDS:FILT_EMB] = -np.sin(fang)
    live = np.arange(TP) < T
    aux = np.zeros((TP, 8))
    aux[:, 0] = tl
    aux[:, 1] = live
    aux[:, 2] = live & (np.arange(TP) >= 1)
    max_decay = math.log(DECAY_TARGET) / FAST_DECAY_PCT
    min_decay = math.log(DECAY_TARGET) / SLOW_DECAY_PCT
    deltas = np.abs(np.linspace(min_decay, max_decay, HY_WIDTH))[None, :]
    return tuple(m.astype(np.float32) for m in (cosp, sinp, zt, aux, deltas))


def _rot_cols(w):
    half = QK_ROPE // 2
    return jnp.concatenate([-w[..., half:], w[..., :half]], axis=-1)


def _rope_gain_rows(g):
    half = QK_ROPE // 2
    gr = g[QK_NOPE:]
    pad = jnp.zeros((LANE - QK_ROPE,), F32)
    return jnp.stack([g[:QK_NOPE], jnp.concatenate([gr, pad]),
                      jnp.concatenate([gr[half:], gr[:half], pad])])


def kernel(x, meta_tokens, norm_mix_g, w_in, q_lat_g, kv_lat_g, w_uq, w_ukv, q_norm_g, k_norm_g, conv_w, conv_b, filt_w1, filt_b1, filt_freq1, filt_w2, filt_b2, filt_freq2, filt_w3, hy_skip, attn_out_g, hy_out_g, w_out, norm_ffn_g, w_gate, w_up, w_down):
    assert x.shape == (1, SEQ, D_MODEL)
    h = jnp.concatenate([meta_tokens.astype(F32), x[0], jnp.zeros((TP - T, D_MODEL), F32)], axis=0)

    cosp, sinp, zt, aux, deltas = (jnp.asarray(m) for m in _position_tables())
    fa, fb, ga = (jnp.asarray(m).astype(BF16) for m in _dft_constants())
    epad = FILT_ORDER - FILT_EMB
    col = lambda v: v[:, None]

    o2, o3 = Q_LORA + KV_LORA, Q_LORA + KV_LORA + QK_ROPE
    w_in_b = w_in.astype(BF16)
    w_in_big = jnp.concatenate(
        [w_in_b[..., :o3], _rot_cols(w_in_b[..., o2:o3]),
         jnp.zeros((DEPTH, D_MODEL, P_ATT - o3 - QK_ROPE), BF16), w_in_b[..., o3:]], axis=-1)
    wq = w_uq.astype(BF16).reshape(DEPTH, Q_LORA, ATT_HEADS, QK_DIM)
    wq = jnp.concatenate([wq, _rot_cols(wq[..., QK_NOPE:])], axis=-1)
    wq = wq.reshape(DEPTH, Q_LORA, ATT_HEADS * HEAD_PAD)
    wkv = w_ukv.astype(BF16)
    w_out_b, w_gate_b, w_up_b, w_down_b = (w.astype(BF16) for w in (w_out, w_gate, w_up, w_down))

    for l in range(DEPTH):
        p = norm_matmul(h, norm_mix_g[l][None], w_in_big, l, 2048)
        q, k, v = mla_prep(p, cosp, sinp, q_lat_g[l][None], kv_lat_g[l][None], wq, wkv, l,
                           _rope_gain_rows(q_norm_g[l]), _rope_gain_rows(k_norm_g[l]))
        att = attention(q, k, v)

        z, u = hyena_pre(p, conv_w[l], conv_b[l][None])
        hf, hb = hyena_filter(
            zt, aux,
            jnp.pad(filt_w1[l].T, ((0, 0), (0, epad))), col(filt_b1[l]), col(filt_freq1[l]),
            filt_w2[l].T, col(filt_b2[l]), col(filt_freq2[l]),
            filt_w3[l].astype(BF16), deltas)
        kf = filter_spectrum(hf, hb, fa, fb)
        y = long_conv(z, kf, fa, fb, ga)

        h = mix_matmul_residual(att, y, z, u, hy_skip[l][None], attn_out_g[l][None], hy_out_g[l][None],
                                w_out_b, l, h, 1024)

        act = norm_swiglu(h, norm_ffn_g[l][None], w_gate_b, w_up_b, l, 512)
        h = matmul_residual(act, w_down_b, l, h, 1024)

    return h[N_META:T][None]
```

```python
import functools
import math

import numpy as np
import jax
import jax.numpy as jnp
from jax import lax
from jax.experimental import pallas as pl
from jax.experimental.pallas import tpu as pltpu

F32 = jnp.float32
BF16 = jnp.bfloat16

D_MODEL = 2048
SEQ = 8192
DEPTH = 4
N_META = 16
T = N_META + SEQ
V_DIM = 128
QK_NOPE = 128
QK_ROPE = 64
QK_DIM = QK_NOPE + QK_ROPE
ATT_HEADS = 8
ATT_WIDTH = ATT_HEADS * V_DIM
Q_LORA = 512
KV_LORA = 256
HY_WIDTH = 1024
FILT_EMB = 33
FILT_BANDS = 16
FILT_ORDER = 64
DECAY_TARGET = 1e-2
FAST_DECAY_PCT = 0.3
SLOW_DECAY_PCT = 1.5
D_FF = 5632
ROPE_THETA = 10000.0
EPS = 1e-6

LANE = 128
HEAD_PAD = 2 * LANE
VMEM_LIMIT = 56 * 1024 * 1024

TP = 8320
TM = 640
TM_FFN = 1664
TQ = 1040
TK = 1792
P_COLS = 4096
P_ATT = 1024
TH = 1024

N2 = 128
N1 = 130
NFFT = N1 * N2
KH = N1 // 2 + 1
KC = 11
NKC = KH // KC
NB = TP // N2
NBP = 72
NCB = HY_WIDTH // LANE
PZ = 136
PA = 136
IM0 = 68
PB = 264


def _cparams(sem, flags=None):
    return pltpu.CompilerParams(dimension_semantics=sem, vmem_limit_bytes=VMEM_LIMIT, flags=flags)


def _rms(x):
    return lax.rsqrt(jnp.mean(x * x, axis=-1, keepdims=True) + EPS)


def _dot(a, b):
    return jnp.dot(a, b, preferred_element_type=F32)


def _norm_mm_kernel(x_ref, g_ref, w_ref, o_ref, xn_ref):
    @pl.when(pl.program_id(1) == 0)
    def _():
        x = x_ref[...]
        xn_ref[...] = (x * _rms(x) * g_ref[...]).astype(BF16)

    o_ref[...] = _dot(xn_ref[...], w_ref[...])


def _layer_spec(l, k, tn):
    return pl.BlockSpec((None, k, tn), lambda i, j: (l, 0, j))


def norm_matmul(x, g, w, l, tn):
    m, k = x.shape
    n = w.shape[2]
    return pl.pallas_call(
        _norm_mm_kernel,
        grid=(m // TM, n // tn),
        in_specs=[pl.BlockSpec((TM, k), lambda i, j: (i, 0)),
                  pl.BlockSpec((1, k), lambda i, j: (0, 0)),
                  _layer_spec(l, k, tn)],
        out_specs=pl.BlockSpec((TM, tn), lambda i, j: (i, j)),
        out_shape=jax.ShapeDtypeStruct((m, n), F32),
        scratch_shapes=[pltpu.VMEM((TM, k), BF16)],
        compiler_params=_cparams(("parallel", "arbitrary")),
        name="norm_matmul",
    )(x, g, w)


def _norm_swiglu_kernel(x_ref, g_ref, wg_ref, wu_ref, o_ref, xn_ref):
    @pl.when(pl.program_id(1) == 0)
    def _():
        x = x_ref[...]
        xn_ref[...] = (x * _rms(x) * g_ref[...]).astype(BF16)

    xn = xn_ref[...]
    a = _dot(xn, wg_ref[...])
    b = _dot(xn, wu_ref[...])
    o_ref[...] = (a * jax.nn.sigmoid(a) * b).astype(BF16)


def norm_swiglu(x, g, wg, wu, l, tn):
    m, k = x.shape
    n = wg.shape[2]
    return pl.pallas_call(
        _norm_swiglu_kernel,
        grid=(m // TM_FFN, n // tn),
        in_specs=[pl.BlockSpec((TM_FFN, k), lambda i, j: (i, 0)),
                  pl.BlockSpec((1, k), lambda i, j: (0, 0)),
                  _layer_spec(l, k, tn),
                  _layer_spec(l, k, tn)],
        out_specs=pl.BlockSpec((TM_FFN, tn), lambda i, j: (i, j)),
        out_shape=jax.ShapeDtypeStruct((m, n), BF16),
        scratch_shapes=[pltpu.VMEM((TM_FFN, k), BF16)],
        compiler_params=_cparams(("parallel", "arbitrary")),
        name="norm_swiglu",
    )(x, g, wg, wu)


def _mm_res_kernel(a_ref, w_ref, r_ref, o_ref):
    o_ref[...] = r_ref[...] + _dot(a_ref[...], w_ref[...])


def matmul_residual(a, w, l, res, tn):
    m, k = a.shape
    n = w.shape[2]
    return pl.pallas_call(
        _mm_res_kernel,
        grid=(m // TM, n // tn),
        in_specs=[pl.BlockSpec((TM, k), lambda i, j: (i, 0)),
                  _layer_spec(l, k, tn),
                  pl.BlockSpec((TM, tn), lambda i, j: (i, j))],
        out_specs=pl.BlockSpec((TM, tn), lambda i, j: (i, j)),
        out_shape=jax.ShapeDtypeStruct((m, n), F32),
        compiler_params=_cparams(("parallel", "arbitrary")),
        name="matmul_residual",
    )(a, w, res)


def _mix_mm_res_kernel(a_ref, y_ref, z_ref, u_ref, d_ref, ga_ref, gy_ref, w_ref, r_ref, o_ref, mix_ref):
    @pl.when(pl.program_id(1) == 0)
    def _():
        a = a_ref[...]
        mix_ref[:, :ATT_WIDTH] = (a * _rms(a) * ga_ref[...]).astype(BF16)
        cat = lambda ref: jnp.concatenate([ref[c] for c in range(NCB)], axis=-1)
        y = (cat(y_ref) + cat(z_ref) * d_ref[...]) * cat(u_ref)
        mix_ref[:, ATT_WIDTH:] = (y * _rms(y) * gy_ref[...]).astype(BF16)

    o_ref[...] = r_ref[...] + _dot(mix_ref[...], w_ref[...])


def mix_matmul_residual(a, y, z, u, d, ga, gy, w, l, res, tn):
    m = a.shape[0]
    n = w.shape[2]
    blk = pl.BlockSpec((NCB, TM, LANE), lambda i, j: (0, i, 0))
    row = lambda width: pl.BlockSpec((1, width), lambda i, j: (0, 0))
    return pl.pallas_call(
        _mix_mm_res_kernel,
        grid=(m // TM, n // tn),
        in_specs=[pl.BlockSpec((TM, ATT_WIDTH), lambda i, j: (i, 0)), blk, blk, blk,
                  row(HY_WIDTH), row(ATT_WIDTH), row(HY_WIDTH),
                  _layer_spec(l, ATT_WIDTH + HY_WIDTH, tn),
                  pl.BlockSpec((TM, tn), lambda i, j: (i, j))],
        out_specs=pl.BlockSpec((TM, tn), lambda i, j: (i, j)),
        out_shape=jax.ShapeDtypeStruct((m, n), F32),
        scratch_shapes=[pltpu.VMEM((TM, ATT_WIDTH + HY_WIDTH), BF16)],
        compiler_params=_cparams(("parallel", "arbitrary")),
        name="mix_matmul_residual",
    )(a, y, z, u, d, ga, gy, w, res)


def _mla_prep_kernel(p_ref, cos_ref, sin_ref, glq_ref, glkv_ref, wq_ref, wkv_ref, gq_ref, gk_ref,
                     q_ref, k_ref, v_ref):
    p = p_ref[...]
    cq = p[:, :Q_LORA]
    ckv = p[:, Q_LORA:Q_LORA + KV_LORA]
    kr = p[:, Q_LORA + KV_LORA:Q_LORA + KV_LORA + LANE]
    qp = _dot((cq * _rms(cq) * glq_ref[...]).astype(BF16), wq_ref[...])
    kvp = _dot((ckv * _rms(ckv) * glkv_ref[...]).astype(BF16), wkv_ref[...])
    cosp = cos_ref[...]
    sinp = sin_ref[...]
    rope_lane = lax.broadcasted_iota(jnp.int32, (1, LANE), 1) < QK_ROPE
    gq = gq_ref[...]
    gk = gk_ref[...]
    q_cos, q_sin = cosp * gq[1:2], sinp * gq[2:3]
    k_cos, k_sin = cosp * gk[1:2], sinp * gk[2:3]
    kr_ss = jnp.sum(jnp.where(rope_lane, kr * kr, 0.0), axis=-1, keepdims=True)
    kr_rot = kr * k_cos + pltpu.roll(kr, QK_ROPE, 1) * k_sin
    q_scale = QK_DIM ** -0.5 * math.log2(math.e)
    for h in range(ATT_HEADS):
        qn = qp[:, h * HEAD_PAD:h * HEAD_PAD + LANE]
        qa = qp[:, h * HEAD_PAD + LANE:(h + 1) * HEAD_PAD]
        ss = (jnp.sum(qn * qn, axis=-1, keepdims=True)
              + jnp.sum(jnp.where(rope_lane, qa * qa, 0.0), axis=-1, keepdims=True))
        rq = lax.rsqrt(ss * (1.0 / QK_DIM) + EPS) * q_scale
        q_ref[h, :, :LANE] = (qn * rq * gq[0:1]).astype(BF16)
        q_ref[h, :, LANE:] = ((qa * q_cos + pltpu.roll(qa, QK_ROPE, 1) * q_sin) * rq).astype(BF16)
        kn = kvp[:, h * HEAD_PAD:h * HEAD_PAD + LANE]
        rk = lax.rsqrt((jnp.sum(kn * kn, axis=-1, keepdims=True) + kr_ss) * (1.0 / QK_DIM) + EPS)
        k_ref[h, :, :LANE] = (kn * rk * gk[0:1]).astype(BF16)
        k_ref[h, :, LANE:] = (kr_rot * rk).astype(BF16)
        v_ref[h] = kvp[:, h * HEAD_PAD + LANE:(h + 1) * HEAD_PAD].astype(BF16)


def mla_prep(p, cosp, sinp, glq, glkv, wq, wkv, l, gq, gk):
    const = lambda i: (0, 0)
    return pl.pallas_call(
        _mla_prep_kernel,
        grid=(TP // TM,),
        in_specs=[pl.BlockSpec((TM, P_ATT), lambda i: (i, 0)),
                  pl.BlockSpec((TM, LANE), lambda i: (i, 0)),
                  pl.BlockSpec((TM, LANE), lambda i: (i, 0)),
                  pl.BlockSpec((1, Q_LORA), const),
                  pl.BlockSpec((1, KV_LORA), const),
                  pl.BlockSpec((None, Q_LORA, ATT_HEADS * HEAD_PAD), lambda i: (l, 0, 0)),
                  pl.BlockSpec((None, KV_LORA, ATT_HEADS * HEAD_PAD), lambda i: (l, 0, 0)),
                  pl.BlockSpec((3, LANE), const),
                  pl.BlockSpec((3, LANE), const)],
        out_specs=[pl.BlockSpec((ATT_HEADS, TM, HEAD_PAD), lambda i: (0, i, 0)),
                   pl.BlockSpec((ATT_HEADS, TM, HEAD_PAD), lambda i: (0, i, 0)),
                   pl.BlockSpec((ATT_HEADS, TM, V_DIM), lambda i: (0, i, 0))],
        out_shape=[jax.ShapeDtypeStruct((ATT_HEADS, TP, HEAD_PAD), BF16),
                   jax.ShapeDtypeStruct((ATT_HEADS, TP, HEAD_PAD), BF16),
                   jax.ShapeDtypeStruct((ATT_HEADS, TP, V_DIM), BF16)],
        compiler_params=_cparams(("parallel",)),
        name="mla_prep",
    )(p, cosp, sinp, glq, glkv, wq, wkv, gq, gk)


def _attn_kernel(q_ref, k_ref, v_ref, o_ref):
    q = q_ref[0]
    m = jnp.full((TQ, 1), -jnp.inf, F32)
    l = jnp.zeros((TQ, 1), F32)
    acc = jnp.zeros((TQ, V_DIM), F32)
    for k0 in range(0, TP, TK):
        k1 = min(k0 + TK, TP)
        s = lax.dot_general(q, k_ref[0, k0:k1, :], (((1,), (1,)), ((), ())), preferred_element_type=F32)
        if k1 > T:
            col = k0 + lax.broadcasted_iota(jnp.int32, (1, k1 - k0), 1)
            s = jnp.where(col < T, s, -jnp.inf)
        m_new = jnp.maximum(m, jnp.max(s, axis=-1, keepdims=True))
        alpha = jnp.exp2(m - m_new)
        pr = jnp.exp2(s - m_new)
        l = alpha * l + jnp.sum(pr, axis=-1, keepdims=True)
        acc = alpha * acc + _dot(pr.astype(BF16), v_ref[0, k0:k1, :])
        m = m_new
    o_ref[...] = acc / l


def attention(q, k, v):
    return pl.pallas_call(
        _attn_kernel,
        grid=(ATT_HEADS, TP // TQ),
        in_specs=[pl.BlockSpec((1, TQ, HEAD_PAD), lambda h, i: (h, i, 0)),
                  pl.BlockSpec((1, TP, HEAD_PAD), lambda h, i: (h, 0, 0)),
                  pl.BlockSpec((1, TP, V_DIM), lambda h, i: (h, 0, 0))],
        out_specs=pl.BlockSpec((TQ, V_DIM), lambda h, i: (i, h)),
        out_shape=jax.ShapeDtypeStruct((TP, ATT_WIDTH), F32),
        compiler_params=_cparams(("parallel", "parallel")),
        name="attention",
    )(q, k, v)


def _short_conv(x, prev_row, next_row, w, b, t0):
    rows = x.shape[0]
    r = lax.broadcasted_iota(jnp.int32, (rows, 1), 0)
    t = t0 + r
    xm = jnp.where(r == 0, prev_row, pltpu.roll(x, 1, 0))
    xm = jnp.where(t == 0, 0.0, xm)
    xp = jnp.where(r == rows - 1, next_row, pltpu.roll(x, rows - 1, 0))
    xp = jnp.where(t >= T - 1, 0.0, xp)
    return xm * w[0:1] + x * w[1:2] + xp * w[2:3] + b


def _hy_pre_kernel(x0_ref, x1_ref, v_ref, x0p_ref, x1p_ref, vp_ref, x0n_ref, x1n_ref, vn_ref,
                   w0_ref, w1_ref, wv_ref, b0_ref, b1_ref, bv_ref, z_ref, u_ref):
    t0 = pl.program_id(0) * TH
    valid = (t0 + lax.broadcasted_iota(jnp.int32, (TH, 1), 0)) < T
    u0 = _short_conv(x0_ref[...], x0p_ref[7:8], x0n_ref[0:1], w0_ref[...], b0_ref[...], t0)
    u1 = _short_conv(x1_ref[...], x1p_ref[7:8], x1n_ref[0:1], w1_ref[...], b1_ref[...], t0)
    uv = _short_conv(v_ref[...], vp_ref[7:8], vn_ref[0:1], wv_ref[...], bv_ref[...], t0)
    z = jnp.where(valid, uv * u1, 0.0)
    u = jnp.where(valid, u0, 0.0)
    for c in range(z_ref.shape[0]):
        z_ref[c] = z[:, c * LANE:(c + 1) * LANE]
        u_ref[c] = u[:, c * LANE:(c + 1) * LANE]


def hyena_pre(p, conv_w, conv_b):
    cw = 512
    ncw = HY_WIDTH // cw
    off = P_ATT // cw
    halo = TH // 8
    last8 = TP // 8 - 1

    def main(s):
        return pl.BlockSpec((TH, cw), lambda i, j: (i, off + s * ncw + j))

    def prev(s):
        return pl.BlockSpec((8, cw), lambda i, j: (jnp.maximum(i * halo - 1, 0), off + s * ncw + j))

    def nxt(s):
        return pl.BlockSpec((8, cw), lambda i, j: (jnp.minimum((i + 1) * halo, last8), off + s * ncw + j))

    def par(s, rows):
        return pl.BlockSpec((rows, cw), lambda i, j: (0, s * ncw + j))

    out_spec = pl.BlockSpec((cw // LANE, TH, LANE), lambda i, j: (j, i, 0))
    out_sds = jax.ShapeDtypeStruct((NCB, TP, LANE), F32)
    return pl.pallas_call(
        _hy_pre_kernel,
        grid=(pl.cdiv(TP, TH), ncw),
        in_specs=[main(0), main(1), main(2), prev(0), prev(1), prev(2), nxt(0), nxt(1), nxt(2),
                  par(0, 3), par(1, 3), par(2, 3), par(0, 1), par(1, 1), par(2, 1)],
        out_specs=[out_spec, out_spec],
        out_shape=[out_sds, out_sds],
        compiler_params=_cparams(("parallel", "parallel")),
        name="hyena_pre",
    )(p, p, p, p, p, p, p, p, p, conv_w, conv_w, conv_w, conv_b, conv_b, conv_b)


def _filter_kernel(zt_ref, aux_ref, w1_ref, b1_ref, f1_ref, w2_ref, b2_ref, f2_ref, w3_ref, dl_ref,
                   hf_ref, hb_ref):
    hi = lax.Precision.HIGHEST
    h = jnp.sin(f1_ref[...] * (jnp.dot(w1_ref[...], zt_ref[...], precision=hi,
                                       preferred_element_type=F32) + b1_ref[...]))
    h = jnp.sin(f2_ref[...] * (jnp.dot(w2_ref[...], h, precision=hi,
                                       preferred_element_type=F32) + b2_ref[...]))
    hh = lax.dot_general(h.astype(BF16), w3_ref[...], (((0,), (0,)), ((), ())),
                         preferred_element_type=F32)
    aux = aux_ref[...]
    decay = jnp.exp(-aux[:, 0:1] * dl_ref[...])
    hf = hh[:, :HY_WIDTH] * (decay * aux[:, 1:2])
    hb = hh[:, HY_WIDTH:] * (decay * aux[:, 2:3])
    for c in range(NCB):
        hf_ref[c] = hf[:, c * LANE:(c + 1) * LANE].astype(BF16)
        hb_ref[c] = hb[:, c * LANE:(c + 1) * LANE].astype(BF16)


def hyena_filter(zt, aux, w1t, b1, f1, w2t, b2, f2, w3, deltas):
    const = lambda i: (0, 0)
    out_spec = pl.BlockSpec((NCB, TM, LANE), lambda i: (0, i, 0))
    out_sds = jax.ShapeDtypeStruct((NCB, TP, LANE), BF16)
    col = pl.BlockSpec((FILT_ORDER, 1), const)
    return pl.pallas_call(
        _filter_kernel,
        grid=(TP // TM,),
        in_specs=[pl.BlockSpec((FILT_ORDER, TM), lambda i: (0, i)),
                  pl.BlockSpec((TM, 8), lambda i: (i, 0)),
                  pl.BlockSpec((FILT_ORDER, FILT_ORDER), const), col, col,
                  pl.BlockSpec((FILT_ORDER, FILT_ORDER), const), col, col,
                  pl.BlockSpec((FILT_ORDER, 2 * HY_WIDTH), const),
                  pl.BlockSpec((1, HY_WIDTH), const)],
        out_specs=[out_spec, out_spec],
        out_shape=[out_sds, out_sds],
        compiler_params=_cparams(("parallel",)),
        name="hyena_filter",
    )(zt, aux, w1t, b1, f1, w2t, b2, f2, w3, deltas)


def _stage_a(x_ref, fa_ref, zs_ref, as_ref):
    for n1 in range(NB):
        zs_ref[n1 * PZ:n1 * PZ + N2, :] = x_ref[0, n1 * N2:(n1 + 1) * N2, :].astype(F32)
    zs_ref[NB * PZ:, :] = jnp.zeros(((NBP - NB) * PZ, LANE), F32)
    kpad = jnp.zeros((fa_ref.shape[1] - NBP, 2 * LANE), F32)

    def body(j, carry):
        n2 = 2 * j
        x = jnp.concatenate([zs_ref[pl.ds(n2, NBP, stride=PZ), :],
                             zs_ref[pl.ds(n2 + 1, NBP, stride=PZ), :]], axis=1)
        r = _dot(fa_ref[...], jnp.concatenate([x, kpad], axis=0).astype(BF16))
        off = pl.multiple_of(n2 * PA, 8)
        as_ref[pl.ds(off, PA), :] = r[:PA, :LANE]
        as_ref[pl.ds(off + PA, PA), :] = r[:PA, LANE:]
        return carry

    lax.fori_loop(0, N2 // 2, body, 0, unroll=4)


def _stage_b_fwd(as_ref, fb, k1):
    ar = as_ref[pl.ds(k1, N2, stride=PA), :]
    ai = as_ref[pl.ds(IM0 + k1, N2, stride=PA), :]
    return _dot(fb, jnp.concatenate([ar, ai], axis=0).astype(BF16))


def _filter_spectrum_kernel(hf_ref, hb_ref, fa_ref, fb_ref, o_ref, zs_ref, af_ref, ab_ref):
    kc = pl.program_id(1)

    @pl.when(kc == 0)
    def _():
        _stage_a(hf_ref, fa_ref, zs_ref, af_ref)
        _stage_a(hb_ref, fa_ref, zs_ref, ab_ref)

    for k in range(KC):
        k1 = kc * KC + k
        sf = _stage_b_fwd(af_ref, fb_ref[k1], k1)
        sb = _stage_b_fwd(ab_ref, fb_ref[k1], k1)
        o_ref[0, k, :N2, :] = sf[:N2] + sb[:N2]
        o_ref[0, k, N2:, :] = sf[N2:] - sb[N2:]


def filter_spectrum(hf, hb, fa, fb):
    blk = pl.BlockSpec((1, TP, LANE), lambda c, k: (c, 0, 0))
    return pl.pallas_call(
        _filter_spectrum_kernel,
        grid=(NCB, NKC),
        in_specs=[blk, blk,
                  pl.BlockSpec(fa.shape, lambda c, k: (0, 0)),
                  pl.BlockSpec(fb.shape, lambda c, k: (0, 0, 0), pipeline_mode=pl.Buffered(1))],
        out_specs=pl.BlockSpec((1, KC, 2 * N2, LANE), lambda c, k: (c, k, 0, 0)),
        out_shape=jax.ShapeDtypeStruct((NCB, KH, 2 * N2, LANE), F32),
        scratch_shapes=[pltpu.VMEM((NBP * PZ, LANE), F32),
                        pltpu.VMEM((N2 * PA, LANE), F32),
                        pltpu.VMEM((N2 * PA, LANE), F32)],
        compiler_params=_cparams(("parallel", "arbitrary")),
        name="filter_spectrum",
    )(hf, hb, fa, fb)


def _long_conv_kernel(z_ref, kf_ref, fa_ref, fb_ref, ga_ref, y_ref, zs_ref, as_ref, bs_ref):
    kc = pl.program_id(1)

    @pl.when(kc == 0)
    def _():
        _stage_a(z_ref, fa_ref, zs_ref, as_ref)
        bs_ref[KH * PB:, :] = jnp.zeros(((NBP - KH) * PB, LANE), F32)

    for k in range(KC):
        k1 = kc * KC + k
        xs = _stage_b_fwd(as_ref, fb_ref[k1], k1)
        xr, xi = xs[:N2], xs[N2:]
        fr, fi = kf_ref[0, k, :N2, :], kf_ref[0, k, N2:, :]
        ys = jnp.concatenate([xr * fr - xi * fi, xr * fi + xi * fr], axis=0).astype(BF16)
        bs_ref[pl.ds(pl.multiple_of(k1 * PB, 8), 2 * N2), :] = lax.dot_general(
            fb_ref[k1], ys, (((0,), (0,)), ((), ())), preferred_element_type=F32)

    @pl.when(kc == NKC - 1)
    def _():
        kpad = jnp.zeros((LANE - NBP, 2 * LANE), F32)

        def body(j, carry):
            n2 = 2 * j
            cr = jnp.concatenate([bs_ref[pl.ds(n2, NBP, stride=PB), :],
                                  bs_ref[pl.ds(n2 + 1, NBP, stride=PB), :]], axis=1)
            ci = jnp.concatenate([bs_ref[pl.ds(N2 + n2, NBP, stride=PB), :],
                                  bs_ref[pl.ds(N2 + n2 + 1, NBP, stride=PB), :]], axis=1)
            c = jnp.concatenate([cr, kpad, ci, kpad], axis=0).astype(BF16)
            y = _dot(ga_ref[...], c)
            zs_ref[pl.ds(n2, NBP, stride=PZ), :] = y[:, :LANE]
            zs_ref[pl.ds(n2 + 1, NBP, stride=PZ), :] = y[:, LANE:]
            return carry

        lax.fori_loop(0, N2 // 2, body, 0, unroll=4)
        for n1 in range(NB):
            y_ref[0, n1 * N2:(n1 + 1) * N2, :] = zs_ref[n1 * PZ:n1 * PZ + N2, :]


def long_conv(z, kf, fa, fb, ga):
    blk = pl.BlockSpec((1, TP, LANE), lambda c, k: (c, 0, 0))
    return pl.pallas_call(
        _long_conv_kernel,
        grid=(NCB, NKC),
        in_specs=[blk,
                  pl.BlockSpec((1, KC, 2 * N2, LANE), lambda c, k: (c, k, 0, 0)),
                  pl.BlockSpec(fa.shape, lambda c, k: (0, 0)),
                  pl.BlockSpec(fb.shape, lambda c, k: (0, 0, 0), pipeline_mode=pl.Buffered(1)),
                  pl.BlockSpec(ga.shape, lambda c, k: (0, 0))],
        out_specs=blk,
        out_shape=jax.ShapeDtypeStruct((NCB, TP, LANE), F32),
        scratch_shapes=[pltpu.VMEM((NBP * PZ, LANE), F32),
                        pltpu.VMEM((N2 * PA, LANE), F32),
                        pltpu.VMEM((NBP * PB, LANE), F32)],
        compiler_params=_cparams(("parallel", "arbitrary")),
        name="long_conv",
    )(z, kf, fa, fb, ga)


@functools.lru_cache(maxsize=None)
def _dft_constants():
    k1 = np.arange(KH, dtype=np.int64)

    n1 = np.arange(NB, dtype=np.int64)
    ang = 2.0 * np.pi * ((k1[:, None] * n1[None, :]) % N1) / N1
    fa = np.zeros((PA + 8, LANE))
    fa[:KH, :NB] = np.cos(ang)
    fa[IM0:IM0 + KH, :NB] = -np.sin(ang)

    n2 = np.arange(N2, dtype=np.int64)
    k2 = np.arange(N2, dtype=np.int64)
    freq = k1[:, None, None] + N1 * k2[None, :, None]
    theta = 2.0 * np.pi * ((freq * n2[None, None, :]) % NFFT) / NFFT
    c, s = np.cos(theta), np.sin(theta)
    fb = np.concatenate([np.concatenate([c, s], axis=2), np.concatenate([-s, c], axis=2)], axis=1)

    herm = np.where((k1 == 0) | (k1 == N1 // 2), 1.0, 2.0) / NFFT
    phi = 2.0 * np.pi * ((n1[:, None] * k1[None, :]) % N1) / N1
    ga = np.zeros((NBP, 2 * LANE))
    ga[:NB, :KH] = np.cos(phi) * herm[None, :]
    ga[:NB, LANE:LANE + KH] = -np.sin(phi) * herm[None, :]
    return tuple(m.astype(np.float32) for m in (fa, fb, ga))


@functools.lru_cache(maxsize=None)
def _position_tables():
    pos = np.arange(TP, dtype=np.float64)
    inv = ROPE_THETA ** (-np.arange(0, QK_ROPE, 2, dtype=np.float64) / QK_ROPE)
    ang = pos[:, None] * inv[None, :]
    ang = np.concatenate([ang, ang], axis=-1)
    pad = np.zeros((TP, LANE - QK_ROPE))
    cosp = np.concatenate([np.cos(ang), pad], axis=-1)
    sinp = np.concatenate([np.sin(ang), pad], axis=-1)

    p = np.minimum(np.arange(TP), T - 1).astype(np.float64)
    tl = p / (T - 1)
    freqs = np.linspace(1e-4, FILT_BANDS - 1, FILT_BANDS)
    fang = (2.0 * np.pi * p / T)[None, :] * freqs[:, None]
    zt = np.zeros((FILT_ORDER, TP))
    zt[0] = tl
    zt[1:1 + FILT_BANDS] = np.cos(fang)
    zt[1 + FILT_BANDS:FILT_EMB] = -np.sin(fang)
    live = np.arange(TP) < T
    aux = np.zeros((TP, 8))
    aux[:, 0] = tl
    aux[:, 1] = live
    aux[:, 2] = live & (np.arange(TP) >= 1)
    max_decay = math.log(DECAY_TARGET) / FAST_DECAY_PCT
    min_decay = math.log(DECAY_TARGET) / SLOW_DECAY_PCT
    deltas = np.abs(np.linspace(min_decay, max_decay, HY_WIDTH))[None, :]
    return tuple(m.astype(np.float32) for m in (cosp, sinp, zt, aux, deltas))


def _rot_cols(w):
    half = QK_ROPE // 2
    return jnp.concatenate([-w[..., half:], w[..., :half]], axis=-1)


def _rope_gain_rows(g):
    half = QK_ROPE // 2
    gr = g[QK_NOPE:]
    pad = jnp.zeros((LANE - QK_ROPE,), F32)
    return jnp.stack([g[:QK_NOPE], jnp.concatenate([gr, pad]),
                      jnp.concatenate([gr[half:], gr[:half], pad])])


def kernel(x, meta_tokens, norm_mix_g, w_in, q_lat_g, kv_lat_g, w_uq, w_ukv, q_norm_g, k_norm_g, conv_w, conv_b, filt_w1, filt_b1, filt_freq1, filt_w2, filt_b2, filt_freq2, filt_w3, hy_skip, attn_out_g, hy_out_g, w_out, norm_ffn_g, w_gate, w_up, w_down):
    assert x.shape == (1, SEQ, D_MODEL)
    h = jnp.concatenate([meta_tokens.astype(F32), x[0], jnp.zeros((TP - T, D_MODEL), F32)], axis=0)

    cosp, sinp, zt, aux, deltas = (jnp.asarray(m) for m in _position_tables())
    fa, fb, ga = (jnp.asarray(m).astype(BF16) for m in _dft_constants())
    epad = FILT_ORDER - FILT_EMB
    col = lambda v: v[:, None]

    o2, o3 = Q_LORA + KV_LORA, Q_LORA + KV_LORA + QK_ROPE
    w_in_b = w_in.astype(BF16)
    w_in_big = jnp.concatenate(
        [w_in_b[..., :o3], _rot_cols(w_in_b[..., o2:o3]),
         jnp.zeros((DEPTH, D_MODEL, P_ATT - o3 - QK_ROPE), BF16), w_in_b[..., o3:]], axis=-1)
    wq = w_uq.astype(BF16).reshape(DEPTH, Q_LORA, ATT_HEADS, QK_DIM)
    wq = jnp.concatenate([wq, _rot_cols(wq[..., QK_NOPE:])], axis=-1)
    wq = wq.reshape(DEPTH, Q_LORA, ATT_HEADS * HEAD_PAD)
    wkv = w_ukv.astype(BF16)
    w_out_b, w_gate_b, w_up_b, w_down_b = (w.astype(BF16) for w in (w_out, w_gate, w_up, w_down))

    for l in range(DEPTH):
        p = norm_matmul(h, norm_mix_g[l][None], w_in_big, l, 2048)
        q, k, v = mla_prep(p, cosp, sinp, q_lat_g[l][None], kv_lat_g[l][None], wq, wkv, l,
                           _rope_gain_rows(q_norm_g[l]), _rope_gain_rows(k_norm_g[l]))
        att = attention(q, k, v)

        z, u = hyena_pre(p, conv_w[l], conv_b[l][None])
        hf, hb = hyena_filter(
            zt, aux,
            jnp.pad(filt_w1[l].T, ((0, 0), (0, epad))), col(filt_b1[l]), col(filt_freq1[l]),
            filt_w2[l].T, col(filt_b2[l]), col(filt_freq2[l]),
            filt_w3[l].astype(BF16), deltas)
        kf = filter_spectrum(hf, hb, fa, fb)
        y = long_conv(z, kf, fa, fb, ga)

        h = mix_matmul_residual(att, y, z, u, hy_skip[l][None], attn_out_g[l][None], hy_out_g[l][None],
                                w_out_b, l, h, 1024)

        act = norm_swiglu(h, norm_ffn_g[l][None], w_gate_b, w_up_b, l, 512)
        h = matmul_residual(act, w_down_b, l, h, 1024)

    return h[N_META:T][None]
```

```python
import functools
import math

import numpy as np
import jax
import jax.numpy as jnp
from jax import lax
from jax.experimental import pallas as pl
from jax.experimental.pallas import tpu as pltpu

F32 = jnp.float32
BF16 = jnp.bfloat16

D_MODEL = 2048
SEQ = 8192
DEPTH = 4
N_META = 16
T = N_META + SEQ
V_DIM = 128
QK_NOPE = 128
QK_ROPE = 64
QK_DIM = QK_NOPE + QK_ROPE
ATT_HEADS = 8
ATT_WIDTH = ATT_HEADS * V_DIM
Q_LORA = 512
KV_LORA = 256
HY_WIDTH = 1024
FILT_EMB = 33
FILT_BANDS = 16
FILT_ORDER = 64
DECAY_TARGET = 1e-2
FAST_DECAY_PCT = 0.3
SLOW_DECAY_PCT = 1.5
D_FF = 5632
ROPE_THETA = 10000.0
EPS = 1e-6

LANE = 128
HEAD_PAD = 2 * LANE
VMEM_LIMIT = 56 * 1024 * 1024

TP = 8320
TM = 640
TM_FFN = 1664
TM_MIX = 320
TQ = 1040
TK = 1792
P_COLS = 4096
P_ATT = 1024
TH = 1024

N2 = 128
N1 = 130
NFFT = N1 * N2
KH = N1 // 2 + 1
KC = 11
NKC = KH // KC
NB = TP // N2
NBP = 72
NCB = HY_WIDTH // LANE
PZ = 136
PA = 136
IM0 = 68
PB = 264


def _cparams(sem, flags=None):
    return pltpu.CompilerParams(dimension_semantics=sem, vmem_limit_bytes=VMEM_LIMIT, flags=flags)


def _rms(x):
    return lax.rsqrt(jnp.mean(x * x, axis=-1, keepdims=True) + EPS)


def _dot(a, b):
    return jnp.dot(a, b, preferred_element_type=F32)


def _norm_mm_kernel(x_ref, g_ref, w_ref, o_ref):
    x = x_ref[...]
    o_ref[...] = _dot((x * _rms(x) * g_ref[...]).astype(BF16), w_ref[...])


def _layer_spec(l, k, tn):
    return pl.BlockSpec((None, k, tn), lambda i, j: (l, 0, j))


def norm_matmul(x, g, w, l, tn):
    m, k = x.shape
    n = w.shape[2]
    return pl.pallas_call(
        _norm_mm_kernel,
        grid=(n // tn, m // TM),
        in_specs=[pl.BlockSpec((TM, k), lambda j, i: (i, 0)),
                  pl.BlockSpec((1, k), lambda j, i: (0, 0)),
                  pl.BlockSpec((None, k, tn), lambda j, i: (l, 0, j))],
        out_specs=pl.BlockSpec((TM, tn), lambda j, i: (i, j)),
        out_shape=jax.ShapeDtypeStruct((m, n), F32),
        compiler_params=_cparams(("parallel", "parallel")),
        name="norm_matmul",
    )(x, g, w)


def _norm_swiglu_kernel(x_ref, g_ref, wg_ref, wu_ref, o_ref, xn_ref):
    @pl.when(pl.program_id(1) == 0)
    def _():
        x = x_ref[...]
        xn_ref[...] = (x * _rms(x) * g_ref[...]).astype(BF16)

    xn = xn_ref[...]
    a = _dot(xn, wg_ref[...])
    b = _dot(xn, wu_ref[...])
    o_ref[...] = (a * jax.nn.sigmoid(a) * b).astype(BF16)


def norm_swiglu(x, g, wg, wu, l, tn):
    m, k = x.shape
    n = wg.shape[2]
    return pl.pallas_call(
        _norm_swiglu_kernel,
        grid=(m // TM_FFN, n // tn),
        in_specs=[pl.BlockSpec((TM_FFN, k), lambda i, j: (i, 0)),
                  pl.BlockSpec((1, k), lambda i, j: (0, 0)),
                  _layer_spec(l, k, tn),
                  _layer_spec(l, k, tn)],
        out_specs=pl.BlockSpec((TM_FFN, tn), lambda i, j: (i, j)),
        out_shape=jax.ShapeDtypeStruct((m, n), BF16),
        scratch_shapes=[pltpu.VMEM((TM_FFN, k), BF16)],
        compiler_params=_cparams(("parallel", "arbitrary")),
        name="norm_swiglu",
    )(x, g, wg, wu)


def _mm_res_kernel(a_ref, w_ref, r_ref, o_ref):
    o_ref[...] = r_ref[...] + _dot(a_ref[...], w_ref[...])


def matmul_residual(a, w, l, res):
    m, k = a.shape
    n = w.shape[2]
    return pl.pallas_call(
        _mm_res_kernel,
        grid=(m // TM_MIX,),
        in_specs=[pl.BlockSpec((TM_MIX, k), lambda i: (i, 0)),
                  pl.BlockSpec((None, k, n), lambda i: (l, 0, 0), pipeline_mode=pl.Buffered(1)),
                  pl.BlockSpec((TM_MIX, n), lambda i: (i, 0))],
        out_specs=pl.BlockSpec((TM_MIX, n), lambda i: (i, 0)),
        out_shape=jax.ShapeDtypeStruct((m, n), F32),
        compiler_params=_cparams(("parallel",)),
        name="matmul_residual",
    )(a, w, res)


def _mix_mm_res_kernel(a_ref, y_ref, z_ref, u_ref, d_ref, ga_ref, gy_ref, w_ref, r_ref, o_ref):
    a = a_ref[...]
    cat = lambda ref: jnp.concatenate([ref[c] for c in range(NCB)], axis=-1)
    y = (cat(y_ref) + cat(z_ref) * d_ref[...]) * cat(u_ref)
    mix = jnp.concatenate([(a * _rms(a) * ga_ref[...]).astype(BF16),
                           (y * _rms(y) * gy_ref[...]).astype(BF16)], axis=-1)
    o_ref[...] = r_ref[...] + _dot(mix, w_ref[...])


def mix_matmul_residual(a, y, z, u, d, ga, gy, w, l, res):
    m = a.shape[0]
    k, n = w.shape[1:]
    blk = pl.BlockSpec((NCB, TM_MIX, LANE), lambda i: (0, i, 0))
    row = lambda width: pl.BlockSpec((1, width), lambda i: (0, 0))
    return pl.pallas_call(
        _mix_mm_res_kernel,
        grid=(m // TM_MIX,),
        in_specs=[pl.BlockSpec((TM_MIX, ATT_WIDTH), lambda i: (i, 0)), blk, blk, blk,
                  row(HY_WIDTH), row(ATT_WIDTH), row(HY_WIDTH),
                  pl.BlockSpec((None, k, n), lambda i: (l, 0, 0), pipeline_mode=pl.Buffered(1)),
                  pl.BlockSpec((TM_MIX, n), lambda i: (i, 0))],
        out_specs=pl.BlockSpec((TM_MIX, n), lambda i: (i, 0)),
        out_shape=jax.ShapeDtypeStruct((m, n), F32),
        compiler_params=_cparams(("parallel",)),
        name="mix_matmul_residual",
    )(a, y, z, u, d, ga, gy, w, res)


def _mla_prep_kernel(p_ref, cos_ref, sin_ref, glq_ref, glkv_ref, wq_ref, wkv_ref, gq_ref, gk_ref,
                     q_ref, k_ref, v_ref):
    p = p_ref[...]
    cq = p[:, :Q_LORA]
    ckv = p[:, Q_LORA:Q_LORA + KV_LORA]
    kr = p[:, Q_LORA + KV_LORA:Q_LORA + KV_LORA + LANE]
    qp = _dot((cq * _rms(cq) * glq_ref[...]).astype(BF16), wq_ref[...])
    kvp = _dot((ckv * _rms(ckv) * glkv_ref[...]).astype(BF16), wkv_ref[...])
    cosp = cos_ref[...]
    sinp = sin_ref[...]
    rope_lane = lax.broadcasted_iota(jnp.int32, (1, LANE), 1) < QK_ROPE
    gq = gq_ref[...]
    gk = gk_ref[...]
    q_cos, q_sin = cosp * gq[1:2], sinp * gq[2:3]
    k_cos, k_sin = cosp * gk[1:2], sinp * gk[2:3]
    kr_ss = jnp.sum(jnp.where(rope_lane, kr * kr, 0.0), axis=-1, keepdims=True)
    kr_rot = kr * k_cos + pltpu.roll(kr, QK_ROPE, 1) * k_sin
    q_scale = QK_DIM ** -0.5 * math.log2(math.e)
    for h in range(ATT_HEADS):
        qn = qp[:, h * HEAD_PAD:h * HEAD_PAD + LANE]
        qa = qp[:, h * HEAD_PAD + LANE:(h + 1) * HEAD_PAD]
        ss = (jnp.sum(qn * qn, axis=-1, keepdims=True)
              + jnp.sum(jnp.where(rope_lane, qa * qa, 0.0), axis=-1, keepdims=True))
        rq = lax.rsqrt(ss * (1.0 / QK_DIM) + EPS) * q_scale
        q_ref[h, :, :LANE] = (qn * rq * gq[0:1]).astype(BF16)
        q_ref[h, :, LANE:] = ((qa * q_cos + pltpu.roll(qa, QK_ROPE, 1) * q_sin) * rq).astype(BF16)
        kn = kvp[:, h * HEAD_PAD:h * HEAD_PAD + LANE]
        rk = lax.rsqrt((jnp.sum(kn * kn, axis=-1, keepdims=True) + kr_ss) * (1.0 / QK_DIM) + EPS)
        k_ref[h, :, :LANE] = (kn * rk * gk[0:1]).astype(BF16)
        k_ref[h, :, LANE:] = (kr_rot * rk).astype(BF16)
        v_ref[h] = kvp[:, h * HEAD_PAD + LANE:(h + 1) * HEAD_PAD].astype(BF16)


def mla_prep(p, cosp, sinp, glq, glkv, wq, wkv, l, gq, gk):
    const = lambda i: (0, 0)
    return pl.pallas_call(
        _mla_prep_kernel,
        grid=(TP // TM,),
        in_specs=[pl.BlockSpec((TM, P_ATT), lambda i: (i, 0)),
                  pl.BlockSpec((TM, LANE), lambda i: (i, 0)),
                  pl.BlockSpec((TM, LANE), lambda i: (i, 0)),
                  pl.BlockSpec((1, Q_LORA), const),
                  pl.BlockSpec((1, KV_LORA), const),
                  pl.BlockSpec((None, Q_LORA, ATT_HEADS * HEAD_PAD), lambda i: (l, 0, 0)),
                  pl.BlockSpec((None, KV_LORA, ATT_HEADS * HEAD_PAD), lambda i: (l, 0, 0)),
                  pl.BlockSpec((3, LANE), const),
                  pl.BlockSpec((3, LANE), const)],
        out_specs=[pl.BlockSpec((ATT_HEADS, TM, HEAD_PAD), lambda i: (0, i, 0)),
                   pl.BlockSpec((ATT_HEADS, TM, HEAD_PAD), lambda i: (0, i, 0)),
                   pl.BlockSpec((ATT_HEADS, TM, V_DIM), lambda i: (0, i, 0))],
        out_shape=[jax.ShapeDtypeStruct((ATT_HEADS, TP, HEAD_PAD), BF16),
                   jax.ShapeDtypeStruct((ATT_HEADS, TP, HEAD_PAD), BF16),
                   jax.ShapeDtypeStruct((ATT_HEADS, TP, V_DIM), BF16)],
        compiler_params=_cparams(("parallel",)),
        name="mla_prep",
    )(p, cosp, sinp, glq, glkv, wq, wkv, gq, gk)


def _attn_kernel(q_ref, k_ref, v_ref, o_ref):
    q = q_ref[0]
    m = jnp.full((TQ, 1), -jnp.inf, F32)
    l = jnp.zeros((TQ, 1), F32)
    acc = jnp.zeros((TQ, V_DIM), F32)
    for k0 in range(0, TP, TK):
        k1 = min(k0 + TK, TP)
        s = lax.dot_general(q, k_ref[0, k0:k1, :], (((1,), (1,)), ((), ())), preferred_element_type=F32)
        if k1 > T:
            col = k0 + lax.broadcasted_iota(jnp.int32, (1, k1 - k0), 1)
            s = jnp.where(col < T, s, -jnp.inf)
        m_new = jnp.maximum(m, jnp.max(s, axis=-1, keepdims=True))
        alpha = jnp.exp2(m - m_new)
        pr = jnp.exp2(s - m_new)
        l = alpha * l + jnp.sum(pr, axis=-1, keepdims=True)
        acc = alpha * acc + _dot(pr.astype(BF16), v_ref[0, k0:k1, :])
        m = m_new
    o_ref[...] = acc / l


def attention(q, k, v):
    return pl.pallas_call(
        _attn_kernel,
        grid=(ATT_HEADS, TP // TQ),
        in_specs=[pl.BlockSpec((1, TQ, HEAD_PAD), lambda h, i: (h, i, 0)),
                  pl.BlockSpec((1, TP, HEAD_PAD), lambda h, i: (h, 0, 0)),
                  pl.BlockSpec((1, TP, V_DIM), lambda h, i: (h, 0, 0))],
        out_specs=pl.BlockSpec((TQ, V_DIM), lambda h, i: (i, h)),
        out_shape=jax.ShapeDtypeStruct((TP, ATT_WIDTH), F32),
        compiler_params=_cparams(("parallel", "parallel")),
        name="attention",
    )(q, k, v)


def _short_conv(x, prev_row, next_row, w, b, t0):
    rows = x.shape[0]
    r = lax.broadcasted_iota(jnp.int32, (rows, 1), 0)
    t = t0 + r
    xm = jnp.where(r == 0, prev_row, pltpu.roll(x, 1, 0))
    xm = jnp.where(t == 0, 0.0, xm)
    xp = jnp.where(r == rows - 1, next_row, pltpu.roll(x, rows - 1, 0))
    xp = jnp.where(t >= T - 1, 0.0, xp)
    return xm * w[0:1] + x * w[1:2] + xp * w[2:3] + b


def _hy_pre_kernel(x0_ref, x1_ref, v_ref, x0p_ref, x1p_ref, vp_ref, x0n_ref, x1n_ref, vn_ref,
                   w0_ref, w1_ref, wv_ref, b0_ref, b1_ref, bv_ref, z_ref, u_ref):
    t0 = pl.program_id(0) * TH
    valid = (t0 + lax.broadcasted_iota(jnp.int32, (TH, 1), 0)) < T
    u0 = _short_conv(x0_ref[...], x0p_ref[7:8], x0n_ref[0:1], w0_ref[...], b0_ref[...], t0)
    u1 = _short_conv(x1_ref[...], x1p_ref[7:8], x1n_ref[0:1], w1_ref[...], b1_ref[...], t0)
    uv = _short_conv(v_ref[...], vp_ref[7:8], vn_ref[0:1], wv_ref[...], bv_ref[...], t0)
    z = jnp.where(valid, uv * u1, 0.0)
    u = jnp.where(valid, u0, 0.0)
    for c in range(z_ref.shape[0]):
        z_ref[c] = z[:, c * LANE:(c + 1) * LANE]
        u_ref[c] = u[:, c * LANE:(c + 1) * LANE]


def hyena_pre(p, conv_w, conv_b):
    cw = 512
    ncw = HY_WIDTH // cw
    off = P_ATT // cw
    halo = TH // 8
    last8 = TP // 8 - 1

    def main(s):
        return pl.BlockSpec((TH, cw), lambda i, j: (i, off + s * ncw + j))

    def prev(s):
        return pl.BlockSpec((8, cw), lambda i, j: (jnp.maximum(i * halo - 1, 0), off + s * ncw + j))

    def nxt(s):
        return pl.BlockSpec((8, cw), lambda i, j: (jnp.minimum((i + 1) * halo, last8), off + s * ncw + j))

    def par(s, rows):
        return pl.BlockSpec((rows, cw), lambda i, j: (0, s * ncw + j))

    out_spec = pl.BlockSpec((cw // LANE, TH, LANE), lambda i, j: (j, i, 0))
    out_sds = jax.ShapeDtypeStruct((NCB, TP, LANE), F32)
    return pl.pallas_call(
        _hy_pre_kernel,
        grid=(pl.cdiv(TP, TH), ncw),
        in_specs=[main(0), main(1), main(2), prev(0), prev(1), prev(2), nxt(0), nxt(1), nxt(2),
                  par(0, 3), par(1, 3), par(2, 3), par(0, 1), par(1, 1), par(2, 1)],
        out_specs=[out_spec, out_spec],
        out_shape=[out_sds, out_sds],
        compiler_params=_cparams(("parallel", "parallel")),
        name="hyena_pre",
    )(p, p, p, p, p, p, p, p, p, conv_w, conv_w, conv_w, conv_b, conv_b, conv_b)


def _filter_kernel(zt_ref, aux_ref, w1_ref, b1_ref, f1_ref, w2_ref, b2_ref, f2_ref, w3_ref, dl_ref,
                   hf_ref, hb_ref):
    hi = lax.Precision.HIGHEST
    h = jnp.sin(f1_ref[...] * (jnp.dot(w1_ref[...], zt_ref[...], precision=hi,
                                       preferred_element_type=F32) + b1_ref[...]))
    h = jnp.sin(f2_ref[...] * (jnp.dot(w2_ref[...], h, precision=hi,
                                       preferred_element_type=F32) + b2_ref[...]))
    hh = lax.dot_general(h.astype(BF16), w3_ref[...], (((0,), (0,)), ((), ())),
                         preferred_element_type=F32)
    aux = aux_ref[...]
    decay = jnp.exp(-aux[:, 0:1] * dl_ref[...])
    hf = hh[:, :HY_WIDTH] * (decay * aux[:, 1:2])
    hb = hh[:, HY_WIDTH:] * (decay * aux[:, 2:3])
    for c in range(NCB):
        hf_ref[c] = hf[:, c * LANE:(c + 1) * LANE].astype(BF16)
        hb_ref[c] = hb[:, c * LANE:(c + 1) * LANE].astype(BF16)


def hyena_filter(zt, aux, w1t, b1, f1, w2t, b2, f2, w3, deltas):
    const = lambda i: (0, 0)
    out_spec = pl.BlockSpec((NCB, TM, LANE), lambda i: (0, i, 0))
    out_sds = jax.ShapeDtypeStruct((NCB, TP, LANE), BF16)
    col = pl.BlockSpec((FILT_ORDER, 1), const)
    return pl.pallas_call(
        _filter_kernel,
        grid=(TP // TM,),
        in_specs=[pl.BlockSpec((FILT_ORDER, TM), lambda i: (0, i)),
                  pl.BlockSpec((TM, 8), lambda i: (i, 0)),
                  pl.BlockSpec((FILT_ORDER, FILT_ORDER), const), col, col,
                  pl.BlockSpec((FILT_ORDER, FILT_ORDER), const), col, col,
                  pl.BlockSpec((FILT_ORDER, 2 * HY_WIDTH), const),
                  pl.BlockSpec((1, HY_WIDTH), const)],
        out_specs=[out_spec, out_spec],
        out_shape=[out_sds, out_sds],
        compiler_params=_cparams(("parallel",)),
        name="hyena_filter",
    )(zt, aux, w1t, b1, f1, w2t, b2, f2, w3, deltas)


def _stage_a(x_ref, fa_ref, zs_ref, as_ref):
    for n1 in range(NB):
        zs_ref[n1 * PZ:n1 * PZ + N2, :] = x_ref[0, n1 * N2:(n1 + 1) * N2, :].astype(F32)
    zs_ref[NB * PZ:, :] = jnp.zeros(((NBP - NB) * PZ, LANE), F32)
    kpad = jnp.zeros((fa_ref.shape[1] - NBP, 2 * LANE), F32)

    def body(j, carry):
        n2 = 2 * j
        x = jnp.concatenate([zs_ref[pl.ds(n2, NBP, stride=PZ), :],
                             zs_ref[pl.ds(n2 + 1, NBP, stride=PZ), :]], axis=1)
        r = _dot(fa_ref[...], jnp.concatenate([x, kpad], axis=0).astype(BF16))
        off = pl.multiple_of(n2 * PA, 8)
        as_ref[pl.ds(off, PA), :] = r[:PA, :LANE]
        as_ref[pl.ds(off + PA, PA), :] = r[:PA, LANE:]
        return carry

    lax.fori_loop(0, N2 // 2, body, 0, unroll=4)


def _stage_b_fwd(as_ref, fb, k1):
    ar = as_ref[pl.ds(k1, N2, stride=PA), :]
    ai = as_ref[pl.ds(IM0 + k1, N2, stride=PA), :]
    return _dot(fb, jnp.concatenate([ar, ai], axis=0).astype(BF16))


def _filter_spectrum_kernel(hf_ref, hb_ref, fa_ref, fb_ref, o_ref, zs_ref, af_ref, ab_ref):
    kc = pl.program_id(1)

    @pl.when(kc == 0)
    def _():
        _stage_a(hf_ref, fa_ref, zs_ref, af_ref)
        _stage_a(hb_ref, fa_ref, zs_ref, ab_ref)

    for k in range(KC):
        k1 = kc * KC + k
        sf = _stage_b_fwd(af_ref, fb_ref[k1], k1)
        sb = _stage_b_fwd(ab_ref, fb_ref[k1], k1)
        o_ref[0, k, :N2, :] = sf[:N2] + sb[:N2]
        o_ref[0, k, N2:, :] = sf[N2:] - sb[N2:]


def filter_spectrum(hf, hb, fa, fb):
    blk = pl.BlockSpec((1, TP, LANE), lambda c, k: (c, 0, 0))
    return pl.pallas_call(
        _filter_spectrum_kernel,
        grid=(NCB, NKC),
        in_specs=[blk, blk,
                  pl.BlockSpec(fa.shape, lambda c, k: (0, 0)),
                  pl.BlockSpec(fb.shape, lambda c, k: (0, 0, 0), pipeline_mode=pl.Buffered(1))],
        out_specs=pl.BlockSpec((1, KC, 2 * N2, LANE), lambda c, k: (c, k, 0, 0)),
        out_shape=jax.ShapeDtypeStruct((NCB, KH, 2 * N2, LANE), F32),
        scratch_shapes=[pltpu.VMEM((NBP * PZ, LANE), F32),
                        pltpu.VMEM((N2 * PA, LANE), F32),
                        pltpu.VMEM((N2 * PA, LANE), F32)],
        compiler_params=_cparams(("parallel", "arbitrary")),
        name="filter_spectrum",
    )(hf, hb, fa, fb)


def _long_conv_kernel(z_ref, kf_ref, fa_ref, fb_ref, ga_ref, y_ref, zs_ref, as_ref, bs_ref):
    kc = pl.program_id(1)

    @pl.when(kc == 0)
    def _():
        _stage_a(z_ref, fa_ref, zs_ref, as_ref)
        bs_ref[KH * PB:, :] = jnp.zeros(((NBP - KH) * PB, LANE), F32)

    for k in range(KC):
        k1 = kc * KC + k
        xs = _stage_b_fwd(as_ref, fb_ref[k1], k1)
        xr, xi = xs[:N2], xs[N2:]
        fr, fi = kf_ref[0, k, :N2, :], kf_ref[0, k, N2:, :]
        ys = jnp.concatenate([xr * fr - xi * fi, xr * fi + xi * fr], axis=0).astype(BF16)
        bs_ref[pl.ds(pl.multiple_of(k1 * PB, 8), 2 * N2), :] = lax.dot_general(
            fb_ref[k1], ys, (((0,), (0,)), ((), ())), preferred_element_type=F32)

    @pl.when(kc == NKC - 1)
    def _():
        kpad = jnp.zeros((LANE - NBP, 2 * LANE), F32)

        def body(j, carry):
            n2 = 2 * j
            cr = jnp.concatenate([bs_ref[pl.ds(n2, NBP, stride=PB), :],
                                  bs_ref[pl.ds(n2 + 1, NBP, stride=PB), :]], axis=1)
            ci = jnp.concatenate([bs_ref[pl.ds(N2 + n2, NBP, stride=PB), :],
                                  bs_ref[pl.ds(N2 + n2 + 1, NBP, stride=PB), :]], axis=1)
            c = jnp.concatenate([cr, kpad, ci, kpad], axis=0).astype(BF16)
            y = _dot(ga_ref[...], c)
            zs_ref[pl.ds(n2, NBP, stride=PZ), :] = y[:, :LANE]
            zs_ref[pl.ds(n2 + 1, NBP, stride=PZ), :] = y[:, LANE:]
            return carry

        lax.fori_loop(0, N2 // 2, body, 0, unroll=4)
        for n1 in range(NB):
            y_ref[0, n1 * N2:(n1 + 1) * N2, :] = zs_ref[n1 * PZ:n1 * PZ + N2, :]


def long_conv(z, kf, fa, fb, ga):
    blk = pl.BlockSpec((1, TP, LANE), lambda c, k: (c, 0, 0))
    return pl.pallas_call(
        _long_conv_kernel,
        grid=(NCB, NKC),
        in_specs=[blk,
                  pl.BlockSpec((1, KC, 2 * N2, LANE), lambda c, k: (c, k, 0, 0)),
                  pl.BlockSpec(fa.shape, lambda c, k: (0, 0)),
                  pl.BlockSpec(fb.shape, lambda c, k: (0, 0, 0), pipeline_mode=pl.Buffered(1)),
                  pl.BlockSpec(ga.shape, lambda c, k: (0, 0))],
        out_specs=blk,
        out_shape=jax.ShapeDtypeStruct((NCB, TP, LANE), F32),
        scratch_shapes=[pltpu.VMEM((NBP * PZ, LANE), F32),
                        pltpu.VMEM((N2 * PA, LANE), F32),
                        pltpu.VMEM((NBP * PB, LANE), F32)],
        compiler_params=_cparams(("parallel", "arbitrary")),
        name="long_conv",
    )(z, kf, fa, fb, ga)


@functools.lru_cache(maxsize=None)
def _dft_constants():
    k1 = np.arange(KH, dtype=np.int64)

    n1 = np.arange(NB, dtype=np.int64)
    ang = 2.0 * np.pi * ((k1[:, None] * n1[None, :]) % N1) / N1
    fa = np.zeros((PA + 8, LANE))
    fa[:KH, :NB] = np.cos(ang)
    fa[IM0:IM0 + KH, :NB] = -np.sin(ang)

    n2 = np.arange(N2, dtype=np.int64)
    k2 = np.arange(N2, dtype=np.int64)
    freq = k1[:, None, None] + N1 * k2[None, :, None]
    theta = 2.0 * np.pi * ((freq * n2[None, None, :]) % NFFT) / NFFT
    c, s = np.cos(theta), np.sin(theta)
    fb = np.concatenate([np.concatenate([c, s], axis=2), np.concatenate([-s, c], axis=2)], axis=1)

    herm = np.where((k1 == 0) | (k1 == N1 // 2), 1.0, 2.0) / NFFT
    phi = 2.0 * np.pi * ((n1[:, None] * k1[None, :]) % N1) / N1
    ga = np.zeros((NBP, 2 * LANE))
    ga[:NB, :KH] = np.cos(phi) * herm[None, :]
    ga[:NB, LANE:LANE + KH] = -np.sin(phi) * herm[None, :]
    return tuple(m.astype(np.float32) for m in (fa, fb, ga))


@functools.lru_cache(maxsize=None)
def _position_tables():
    pos = np.arange(TP, dtype=np.float64)
    inv = ROPE_THETA ** (-np.arange(0, QK_ROPE, 2, dtype=np.float64) / QK_ROPE)
    ang = pos[:, None] * inv[None, :]
    ang = np.concatenate([ang, ang], axis=-1)
    pad = np.zeros((TP, LANE - QK_ROPE))
    cosp = np.concatenate([np.cos(ang), pad], axis=-1)
    sinp = np.concatenate([np.sin(ang), pad], axis=-1)

    p = np.minimum(np.arange(TP), T - 1).astype(np.float64)
    tl = p / (T - 1)
    freqs = np.linspace(1e-4, FILT_BANDS - 1, FILT_BANDS)
    fang = (2.0 * np.pi * p / T)[None, :] * freqs[:, None]
    zt = np.zeros((FILT_ORDER, TP))
    zt[0] = tl
    zt[1:1 + FILT_BANDS] = np.cos(fang)
    zt[1 + FILT_BANDS:FILT_EMB] = -np.sin(fang)
    live = np.arange(TP) < T
    aux = np.zeros((TP, 8))
    aux[:, 0] = tl
    aux[:, 1] = live
    aux[:, 2] = live & (np.arange(TP) >= 1)
    max_decay = math.log(DECAY_TARGET) / FAST_DECAY_PCT
    min_decay = math.log(DECAY_TARGET) / SLOW_DECAY_PCT
    deltas = np.abs(np.linspace(min_decay, max_decay, HY_WIDTH))[None, :]
    return tuple(m.astype(np.float32) for m in (cosp, sinp, zt, aux, deltas))


def _rot_cols(w):
    half = QK_ROPE // 2
    return jnp.concatenate([-w[..., half:], w[..., :half]], axis=-1)


def _rope_gain_rows(g):
    half = QK_ROPE // 2
    gr = g[QK_NOPE:]
    pad = jnp.zeros((LANE - QK_ROPE,), F32)
    return jnp.stack([g[:QK_NOPE], jnp.concatenate([gr, pad]),
                      jnp.concatenate([gr[half:], gr[:half], pad])])


def kernel(x, meta_tokens, norm_mix_g, w_in, q_lat_g, kv_lat_g, w_uq, w_ukv, q_norm_g, k_norm_g, conv_w, conv_b, filt_w1, filt_b1, filt_freq1, filt_w2, filt_b2, filt_freq2, filt_w3, hy_skip, attn_out_g, hy_out_g, w_out, norm_ffn_g, w_gate, w_up, w_down):
    assert x.shape == (1, SEQ, D_MODEL)
    h = jnp.concatenate([meta_tokens.astype(F32), x[0], jnp.zeros((TP - T, D_MODEL), F32)], axis=0)

    cosp, sinp, zt, aux, deltas = (jnp.asarray(m) for m in _position_tables())
    fa, fb, ga = (jnp.asarray(m).astype(BF16) for m in _dft_constants())
    epad = FILT_ORDER - FILT_EMB
    col = lambda v: v[:, None]

    o2, o3 = Q_LORA + KV_LORA, Q_LORA + KV_LORA + QK_ROPE
    w_in_b = w_in.astype(BF16)
    w_in_big = jnp.concatenate(
        [w_in_b[..., :o3], _rot_cols(w_in_b[..., o2:o3]),
         jnp.zeros((DEPTH, D_MODEL, P_ATT - o3 - QK_ROPE), BF16), w_in_b[..., o3:]], axis=-1)
    wq = w_uq.astype(BF16).reshape(DEPTH, Q_LORA, ATT_HEADS, QK_DIM)
    wq = jnp.concatenate([wq, _rot_cols(wq[..., QK_NOPE:])], axis=-1)
    wq = wq.reshape(DEPTH, Q_LORA, ATT_HEADS * HEAD_PAD)
    wkv = w_ukv.astype(BF16)
    w_out_b, w_gate_b, w_up_b, w_down_b = (w.astype(BF16) for w in (w_out, w_gate, w_up, w_down))

    for l in range(DEPTH):
        p = norm_matmul(h, norm_mix_g[l][None], w_in_big, l, 2048)
        q, k, v = mla_prep(p, cosp, sinp, q_lat_g[l][None], kv_lat_g[l][None], wq, wkv, l,
                           _rope_gain_rows(q_norm_g[l]), _rope_gain_rows(k_norm_g[l]))
        att = attention(q, k, v)

        z, u = hyena_pre(p, conv_w[l], conv_b[l][None])
        hf, hb = hyena_filter(
            zt, aux,
            jnp.pad(filt_w1[l].T, ((0, 0), (0, epad))), col(filt_b1[l]), col(filt_freq1[l]),
            filt_w2[l].T, col(filt_b2[l]), col(filt_freq2[l]),
            filt_w3[l].astype(BF16), deltas)
        kf = filter_spectrum(hf, hb, fa, fb)
        y = long_conv(z, kf, fa, fb, ga)

        h = mix_matmul_residual(att, y, z, u, hy_skip[l][None], attn_out_g[l][None], hy_out_g[l][None],
                                w_out_b, l, h)

        act = norm_swiglu(h, norm_ffn_g[l][None], w_gate_b, w_up_b, l, 512)
        h = matmul_residual(act, w_down_b, l, h)

    return h[N_META:T][None]
```

```python
import functools
import math

import numpy as np
import jax
import jax.numpy as jnp
from jax import lax
from jax.experimental import pallas as pl
from jax.experimental.pallas import tpu as pltpu

F32 = jnp.float32
BF16 = jnp.bfloat16

D_MODEL = 2048
SEQ = 8192
DEPTH = 4
N_META = 16
T = N_META + SEQ
V_DIM = 128
QK_NOPE = 128
QK_ROPE = 64
QK_DIM = QK_NOPE + QK_ROPE
ATT_HEADS = 8
ATT_WIDTH = ATT_HEADS * V_DIM
Q_LORA = 512
KV_LORA = 256
HY_WIDTH = 1024
FILT_EMB = 33
FILT_BANDS = 16
FILT_ORDER = 64
DECAY_TARGET = 1e-2
FAST_DECAY_PCT = 0.3
SLOW_DECAY_PCT = 1.5
D_FF = 5632
ROPE_THETA = 10000.0
EPS = 1e-6

LANE = 128
HEAD_PAD = 2 * LANE
VMEM_LIMIT = 56 * 1024 * 1024

TP = 8320
TM = 640
TM_FFN = 1664
TM_MIX = 320
TQ = 1040
TK = 1792
P_COLS = 4096
P_ATT = 1024
TH = 1024

N2 = 128
N1 = 130
NFFT = N1 * N2
KH = N1 // 2 + 1
KC = 22
NKC = KH // KC
NB = TP // N2
NBP = 72
NCB = HY_WIDTH // LANE
PZ = 136
PA = 136
IM0 = 68
PB = 264


def _cparams(sem, flags=None):
    return pltpu.CompilerParams(dimension_semantics=sem, vmem_limit_bytes=VMEM_LIMIT, flags=flags)


def _rms(x):
    return lax.rsqrt(jnp.mean(x * x, axis=-1, keepdims=True) + EPS)


def _dot(a, b):
    return jnp.dot(a, b, preferred_element_type=F32)


def _norm_mm_kernel(x_ref, g_ref, w_ref, o_ref):
    x = x_ref[...]
    o_ref[...] = _dot((x * _rms(x) * g_ref[...]).astype(BF16), w_ref[...])


def _layer_spec(l, k, tn):
    return pl.BlockSpec((None, k, tn), lambda i, j: (l, 0, j))


def norm_matmul(x, g, w, l, tn):
    m, k = x.shape
    n = w.shape[2]
    return pl.pallas_call(
        _norm_mm_kernel,
        grid=(n // tn, m // TM),
        in_specs=[pl.BlockSpec((TM, k), lambda j, i: (i, 0)),
                  pl.BlockSpec((1, k), lambda j, i: (0, 0)),
                  pl.BlockSpec((None, k, tn), lambda j, i: (l, 0, j))],
        out_specs=pl.BlockSpec((TM, tn), lambda j, i: (i, j)),
        out_shape=jax.ShapeDtypeStruct((m, n), F32),
        compiler_params=_cparams(("parallel", "parallel")),
        name="norm_matmul",
    )(x, g, w)


def _norm_swiglu_kernel(x_ref, g_ref, wg_ref, wu_ref, o_ref, xn_ref):
    @pl.when(pl.program_id(1) == 0)
    def _():
        x = x_ref[...]
        xn_ref[...] = (x * _rms(x) * g_ref[...]).astype(BF16)

    xn = xn_ref[...]
    a = _dot(xn, wg_ref[...])
    b = _dot(xn, wu_ref[...])
    o_ref[...] = (a * jax.nn.sigmoid(a) * b).astype(BF16)


def norm_swiglu(x, g, wg, wu, l, tn):
    m, k = x.shape
    n = wg.shape[2]
    return pl.pallas_call(
        _norm_swiglu_kernel,
        grid=(m // TM_FFN, n // tn),
        in_specs=[pl.BlockSpec((TM_FFN, k), lambda i, j: (i, 0)),
                  pl.BlockSpec((1, k), lambda i, j: (0, 0)),
                  _layer_spec(l, k, tn),
                  _layer_spec(l, k, tn)],
        out_specs=pl.BlockSpec((TM_FFN, tn), lambda i, j: (i, j)),
        out_shape=jax.ShapeDtypeStruct((m, n), BF16),
        scratch_shapes=[pltpu.VMEM((TM_FFN, k), BF16)],
        compiler_params=_cparams(("parallel", "arbitrary")),
        name="norm_swiglu",
    )(x, g, wg, wu)


def _mm_res_kernel(a_ref, w_ref, r_ref, o_ref):
    o_ref[...] = r_ref[...] + _dot(a_ref[...], w_ref[...])


def matmul_residual(a, w, l, res):
    m, k = a.shape
    n = w.shape[2]
    return pl.pallas_call(
        _mm_res_kernel,
        grid=(m // TM_MIX,),
        in_specs=[pl.BlockSpec((TM_MIX, k), lambda i: (i, 0)),
                  pl.BlockSpec((None, k, n), lambda i: (l, 0, 0), pipeline_mode=pl.Buffered(1)),
                  pl.BlockSpec((TM_MIX, n), lambda i: (i, 0))],
        out_specs=pl.BlockSpec((TM_MIX, n), lambda i: (i, 0)),
        out_shape=jax.ShapeDtypeStruct((m, n), F32),
        compiler_params=_cparams(("parallel",)),
        name="matmul_residual",
    )(a, w, res)


def _mix_mm_res_kernel(a_ref, y_ref, z_ref, u_ref, d_ref, ga_ref, gy_ref, w_ref, r_ref, o_ref):
    a = a_ref[...]
    cat = lambda ref: jnp.concatenate([ref[c] for c in range(NCB)], axis=-1)
    y = (cat(y_ref) + cat(z_ref) * d_ref[...]) * cat(u_ref)
    mix = jnp.concatenate([(a * _rms(a) * ga_ref[...]).astype(BF16),
                           (y * _rms(y) * gy_ref[...]).astype(BF16)], axis=-1)
    o_ref[...] = r_ref[...] + _dot(mix, w_ref[...])


def mix_matmul_residual(a, y, z, u, d, ga, gy, w, l, res):
    m = a.shape[0]
    k, n = w.shape[1:]
    blk = pl.BlockSpec((NCB, TM_MIX, LANE), lambda i: (0, i, 0))
    row = lambda width: pl.BlockSpec((1, width), lambda i: (0, 0))
    return pl.pallas_call(
        _mix_mm_res_kernel,
        grid=(m // TM_MIX,),
        in_specs=[pl.BlockSpec((TM_MIX, ATT_WIDTH), lambda i: (i, 0)), blk, blk, blk,
                  row(HY_WIDTH), row(ATT_WIDTH), row(HY_WIDTH),
                  pl.BlockSpec((None, k, n), lambda i: (l, 0, 0), pipeline_mode=pl.Buffered(1)),
                  pl.BlockSpec((TM_MIX, n), lambda i: (i, 0))],
        out_specs=pl.BlockSpec((TM_MIX, n), lambda i: (i, 0)),
        out_shape=jax.ShapeDtypeStruct((m, n), F32),
        compiler_params=_cparams(("parallel",)),
        name="mix_matmul_residual",
    )(a, y, z, u, d, ga, gy, w, res)


def _mla_prep_kernel(p_ref, cos_ref, sin_ref, glq_ref, glkv_ref, wq_ref, wkv_ref, gq_ref, gk_ref,
                     q_ref, k_ref, v_ref):
    p = p_ref[...]
    cq = p[:, :Q_LORA]
    ckv = p[:, Q_LORA:Q_LORA + KV_LORA]
    kr = p[:, Q_LORA + KV_LORA:Q_LORA + KV_LORA + LANE]
    qp = _dot((cq * _rms(cq) * glq_ref[...]).astype(BF16), wq_ref[...])
    kvp = _dot((ckv * _rms(ckv) * glkv_ref[...]).astype(BF16), wkv_ref[...])
    cosp = cos_ref[...]
    sinp = sin_ref[...]
    rope_lane = lax.broadcasted_iota(jnp.int32, (1, LANE), 1) < QK_ROPE
    gq = gq_ref[...]
    gk = gk_ref[...]
    q_cos, q_sin = cosp * gq[1:2], sinp * gq[2:3]
    k_cos, k_sin = cosp * gk[1:2], sinp * gk[2:3]
    kr_ss = jnp.sum(jnp.where(rope_lane, kr * kr, 0.0), axis=-1, keepdims=True)
    kr_rot = kr * k_cos + pltpu.roll(kr, QK_ROPE, 1) * k_sin
    q_scale = QK_DIM ** -0.5 * math.log2(math.e)
    for h in range(ATT_HEADS):
        qn = qp[:, h * HEAD_PAD:h * HEAD_PAD + LANE]
        qa = qp[:, h * HEAD_PAD + LANE:(h + 1) * HEAD_PAD]
        ss = (jnp.sum(qn * qn, axis=-1, keepdims=True)
              + jnp.sum(jnp.where(rope_lane, qa * qa, 0.0), axis=-1, keepdims=True))
        rq = lax.rsqrt(ss * (1.0 / QK_DIM) + EPS) * q_scale
        q_ref[h, :, :LANE] = (qn * rq * gq[0:1]).astype(BF16)
        q_ref[h, :, LANE:] = ((qa * q_cos + pltpu.roll(qa, QK_ROPE, 1) * q_sin) * rq).astype(BF16)
        kn = kvp[:, h * HEAD_PAD:h * HEAD_PAD + LANE]
        rk = lax.rsqrt((jnp.sum(kn * kn, axis=-1, keepdims=True) + kr_ss) * (1.0 / QK_DIM) + EPS)
        k_ref[h, :, :LANE] = (kn * rk * gk[0:1]).astype(BF16)
        k_ref[h, :, LANE:] = (kr_rot * rk).astype(BF16)
        v_ref[h] = kvp[:, h * HEAD_PAD + LANE:(h + 1) * HEAD_PAD].astype(BF16)


def mla_prep(p, cosp, sinp, glq, glkv, wq, wkv, l, gq, gk):
    const = lambda i: (0, 0)
    return pl.pallas_call(
        _mla_prep_kernel,
        grid=(TP // TM,),
        in_specs=[pl.BlockSpec((TM, P_ATT), lambda i: (i, 0)),
                  pl.BlockSpec((TM, LANE), lambda i: (i, 0)),
                  pl.BlockSpec((TM, LANE), lambda i: (i, 0)),
                  pl.BlockSpec((1, Q_LORA), const),
                  pl.BlockSpec((1, KV_LORA), const),
                  pl.BlockSpec((None, Q_LORA, ATT_HEADS * HEAD_PAD), lambda i: (l, 0, 0)),
                  pl.BlockSpec((None, KV_LORA, ATT_HEADS * HEAD_PAD), lambda i: (l, 0, 0)),
                  pl.BlockSpec((3, LANE), const),
                  pl.BlockSpec((3, LANE), const)],
        out_specs=[pl.BlockSpec((ATT_HEADS, TM, HEAD_PAD), lambda i: (0, i, 0)),
                   pl.BlockSpec((ATT_HEADS, TM, HEAD_PAD), lambda i: (0, i, 0)),
                   pl.BlockSpec((ATT_HEADS, TM, V_DIM), lambda i: (0, i, 0))],
        out_shape=[jax.ShapeDtypeStruct((ATT_HEADS, TP, HEAD_PAD), BF16),
                   jax.ShapeDtypeStruct((ATT_HEADS, TP, HEAD_PAD), BF16),
                   jax.ShapeDtypeStruct((ATT_HEADS, TP, V_DIM), BF16)],
        compiler_params=_cparams(("parallel",)),
        name="mla_prep",
    )(p, cosp, sinp, glq, glkv, wq, wkv, gq, gk)


def _attn_kernel(q_ref, k_ref, v_ref, o_ref):
    q = q_ref[0]
    m = jnp.full((TQ, 1), -jnp.inf, F32)
    l = jnp.zeros((TQ, 1), F32)
    acc = jnp.zeros((TQ, V_DIM), F32)
    for k0 in range(0, TP, TK):
        k1 = min(k0 + TK, TP)
        s = lax.dot_general(q, k_ref[0, k0:k1, :], (((1,), (1,)), ((), ())), preferred_element_type=F32)
        if k1 > T:
            col = k0 + lax.broadcasted_iota(jnp.int32, (1, k1 - k0), 1)
            s = jnp.where(col < T, s, -jnp.inf)
        m_new = jnp.maximum(m, jnp.max(s, axis=-1, keepdims=True))
        alpha = jnp.exp2(m - m_new)
        pr = jnp.exp2(s - m_new)
        l = alpha * l + jnp.sum(pr, axis=-1, keepdims=True)
        acc = alpha * acc + _dot(pr.astype(BF16), v_ref[0, k0:k1, :])
        m = m_new
    o_ref[...] = acc / l


def attention(q, k, v):
    return pl.pallas_call(
        _attn_kernel,
        grid=(ATT_HEADS, TP // TQ),
        in_specs=[pl.BlockSpec((1, TQ, HEAD_PAD), lambda h, i: (h, i, 0)),
                  pl.BlockSpec((1, TP, HEAD_PAD), lambda h, i: (h, 0, 0)),
                  pl.BlockSpec((1, TP, V_DIM), lambda h, i: (h, 0, 0))],
        out_specs=pl.BlockSpec((TQ, V_DIM), lambda h, i: (i, h)),
        out_shape=jax.ShapeDtypeStruct((TP, ATT_WIDTH), F32),
        compiler_params=_cparams(("parallel", "parallel")),
        name="attention",
    )(q, k, v)


def _short_conv(x, prev_row, next_row, w, b, t0):
    rows = x.shape[0]
    r = lax.broadcasted_iota(jnp.int32, (rows, 1), 0)
    t = t0 + r
    xm = jnp.where(r == 0, prev_row, pltpu.roll(x, 1, 0))
    xm = jnp.where(t == 0, 0.0, xm)
    xp = jnp.where(r == rows - 1, next_row, pltpu.roll(x, rows - 1, 0))
    xp = jnp.where(t >= T - 1, 0.0, xp)
    return xm * w[0:1] + x * w[1:2] + xp * w[2:3] + b


def _hy_pre_kernel(x0_ref, x1_ref, v_ref, x0p_ref, x1p_ref, vp_ref, x0n_ref, x1n_ref, vn_ref,
                   w0_ref, w1_ref, wv_ref, b0_ref, b1_ref, bv_ref, z_ref, u_ref):
    t0 = pl.program_id(0) * TH
    valid = (t0 + lax.broadcasted_iota(jnp.int32, (TH, 1), 0)) < T
    u0 = _short_conv(x0_ref[...], x0p_ref[7:8], x0n_ref[0:1], w0_ref[...], b0_ref[...], t0)
    u1 = _short_conv(x1_ref[...], x1p_ref[7:8], x1n_ref[0:1], w1_ref[...], b1_ref[...], t0)
    uv = _short_conv(v_ref[...], vp_ref[7:8], vn_ref[0:1], wv_ref[...], bv_ref[...], t0)
    z = jnp.where(valid, uv * u1, 0.0)
    u = jnp.where(valid, u0, 0.0)
    for c in range(z_ref.shape[0]):
        z_ref[c] = z[:, c * LANE:(c + 1) * LANE]
        u_ref[c] = u[:, c * LANE:(c + 1) * LANE]


def hyena_pre(p, conv_w, conv_b):
    cw = 512
    ncw = HY_WIDTH // cw
    off = P_ATT // cw
    halo = TH // 8
    last8 = TP // 8 - 1

    def main(s):
        return pl.BlockSpec((TH, cw), lambda i, j: (i, off + s * ncw + j))

    def prev(s):
        return pl.BlockSpec((8, cw), lambda i, j: (jnp.maximum(i * halo - 1, 0), off + s * ncw + j))

    def nxt(s):
        return pl.BlockSpec((8, cw), lambda i, j: (jnp.minimum((i + 1) * halo, last8), off + s * ncw + j))

    def par(s, rows):
        return pl.BlockSpec((rows, cw), lambda i, j: (0, s * ncw + j))

    out_spec = pl.BlockSpec((cw // LANE, TH, LANE), lambda i, j: (j, i, 0))
    out_sds = jax.ShapeDtypeStruct((NCB, TP, LANE), F32)
    return pl.pallas_call(
        _hy_pre_kernel,
        grid=(pl.cdiv(TP, TH), ncw),
        in_specs=[main(0), main(1), main(2), prev(0), prev(1), prev(2), nxt(0), nxt(1), nxt(2),
                  par(0, 3), par(1, 3), par(2, 3), par(0, 1), par(1, 1), par(2, 1)],
        out_specs=[out_spec, out_spec],
        out_shape=[out_sds, out_sds],
        compiler_params=_cparams(("parallel", "parallel")),
        name="hyena_pre",
    )(p, p, p, p, p, p, p, p, p, conv_w, conv_w, conv_w, conv_b, conv_b, conv_b)


def _filter_kernel(zt_ref, aux_ref, w1_ref, b1_ref, f1_ref, w2_ref, b2_ref, f2_ref, w3_ref, dl_ref,
                   hf_ref, hb_ref):
    hi = lax.Precision.HIGHEST
    h = jnp.sin(f1_ref[...] * (jnp.dot(w1_ref[...], zt_ref[...], precision=hi,
                                       preferred_element_type=F32) + b1_ref[...]))
    h = jnp.sin(f2_ref[...] * (jnp.dot(w2_ref[...], h, precision=hi,
                                       preferred_element_type=F32) + b2_ref[...]))
    hh = lax.dot_general(h.astype(BF16), w3_ref[...], (((0,), (0,)), ((), ())),
                         preferred_element_type=F32)
    aux = aux_ref[...]
    decay = jnp.exp(-aux[:, 0:1] * dl_ref[...])
    hf = hh[:, :HY_WIDTH] * (decay * aux[:, 1:2])
    hb = hh[:, HY_WIDTH:] * (decay * aux[:, 2:3])
    for c in range(NCB):
        hf_ref[c] = hf[:, c * LANE:(c + 1) * LANE].astype(BF16)
        hb_ref[c] = hb[:, c * LANE:(c + 1) * LANE].astype(BF16)


def hyena_filter(zt, aux, w1t, b1, f1, w2t, b2, f2, w3, deltas):
    const = lambda i: (0, 0)
    out_spec = pl.BlockSpec((NCB, TM, LANE), lambda i: (0, i, 0))
    out_sds = jax.ShapeDtypeStruct((NCB, TP, LANE), BF16)
    col = pl.BlockSpec((FILT_ORDER, 1), const)
    return pl.pallas_call(
        _filter_kernel,
        grid=(TP // TM,),
        in_specs=[pl.BlockSpec((FILT_ORDER, TM), lambda i: (0, i)),
                  pl.BlockSpec((TM, 8), lambda i: (i, 0)),
                  pl.BlockSpec((FILT_ORDER, FILT_ORDER), const), col, col,
                  pl.BlockSpec((FILT_ORDER, FILT_ORDER), const), col, col,
                  pl.BlockSpec((FILT_ORDER, 2 * HY_WIDTH), const),
                  pl.BlockSpec((1, HY_WIDTH), const)],
        out_specs=[out_spec, out_spec],
        out_shape=[out_sds, out_sds],
        compiler_params=_cparams(("parallel",)),
        name="hyena_filter",
    )(zt, aux, w1t, b1, f1, w2t, b2, f2, w3, deltas)


def _stage_a(x_ref, fa_ref, zs_ref, as_ref):
    for n1 in range(NB):
        zs_ref[n1 * PZ:n1 * PZ + N2, :] = x_ref[0, n1 * N2:(n1 + 1) * N2, :].astype(F32)
    zs_ref[NB * PZ:, :] = jnp.zeros(((NBP - NB) * PZ, LANE), F32)
    kpad = jnp.zeros((fa_ref.shape[1] - NBP, 2 * LANE), F32)

    def body(j, carry):
        n2 = 2 * j
        x = jnp.concatenate([zs_ref[pl.ds(n2, NBP, stride=PZ), :],
                             zs_ref[pl.ds(n2 + 1, NBP, stride=PZ), :]], axis=1)
        r = _dot(fa_ref[...], jnp.concatenate([x, kpad], axis=0).astype(BF16))
        off = pl.multiple_of(n2 * PA, 8)
        as_ref[pl.ds(off, PA), :] = r[:PA, :LANE]
        as_ref[pl.ds(off + PA, PA), :] = r[:PA, LANE:]
        return carry

    lax.fori_loop(0, N2 // 2, body, 0, unroll=4)


def _stage_b_fwd(as_ref, fb, k1):
    ar = as_ref[pl.ds(k1, N2, stride=PA), :]
    ai = as_ref[pl.ds(IM0 + k1, N2, stride=PA), :]
    return _dot(fb, jnp.concatenate([ar, ai], axis=0).astype(BF16))


def _filter_spectrum_kernel(hf_ref, hb_ref, fa_ref, fb_ref, o_ref, zs_ref, af_ref, ab_ref):
    kc = pl.program_id(1)

    @pl.when(kc == 0)
    def _():
        _stage_a(hf_ref, fa_ref, zs_ref, af_ref)
        _stage_a(hb_ref, fa_ref, zs_ref, ab_ref)

    for k in range(KC):
        k1 = kc * KC + k
        sf = _stage_b_fwd(af_ref, fb_ref[k1], k1)
        sb = _stage_b_fwd(ab_ref, fb_ref[k1], k1)
        o_ref[0, k, :N2, :] = (sf[:N2] + sb[:N2]).astype(BF16)
        o_ref[0, k, N2:, :] = (sf[N2:] - sb[N2:]).astype(BF16)


def filter_spectrum(hf, hb, fa, fb):
    blk = pl.BlockSpec((1, TP, LANE), lambda c, k: (c, 0, 0))
    return pl.pallas_call(
        _filter_spectrum_kernel,
        grid=(NCB, NKC),
        in_specs=[blk, blk,
                  pl.BlockSpec(fa.shape, lambda c, k: (0, 0)),
                  pl.BlockSpec(fb.shape, lambda c, k: (0, 0, 0), pipeline_mode=pl.Buffered(1))],
        out_specs=pl.BlockSpec((1, KC, 2 * N2, LANE), lambda c, k: (c, k, 0, 0)),
        out_shape=jax.ShapeDtypeStruct((NCB, KH, 2 * N2, LANE), BF16),
        scratch_shapes=[pltpu.VMEM((NBP * PZ, LANE), F32),
                        pltpu.VMEM((N2 * PA, LANE), F32),
                        pltpu.VMEM((N2 * PA, LANE), F32)],
        compiler_params=_cparams(("parallel", "arbitrary")),
        name="filter_spectrum",
    )(hf, hb, fa, fb)


def _long_conv_kernel(z_ref, kf_ref, fa_ref, fb_ref, ga_ref, y_ref, zs_ref, as_ref, bs_ref):
    kc = pl.program_id(1)

    @pl.when(kc == 0)
    def _():
        _stage_a(z_ref, fa_ref, zs_ref, as_ref)
        bs_ref[KH * PB:, :] = jnp.zeros(((NBP - KH) * PB, LANE), F32)

    for k in range(KC):
        k1 = kc * KC + k
        xs = _stage_b_fwd(as_ref, fb_ref[k1], k1)
        xr, xi = xs[:N2], xs[N2:]
        fr, fi = kf_ref[0, k, :N2, :].astype(F32), kf_ref[0, k, N2:, :].astype(F32)
        ys = jnp.concatenate([xr * fr - xi * fi, xr * fi + xi * fr], axis=0).astype(BF16)
        bs_ref[pl.ds(pl.multiple_of(k1 * PB, 8), 2 * N2), :] = lax.dot_general(
            fb_ref[k1], ys, (((0,), (0,)), ((), ())), preferred_element_type=F32)

    @pl.when(kc == NKC - 1)
    def _():
        kpad = jnp.zeros((LANE - NBP, 2 * LANE), F32)

        def body(j, carry):
            n2 = 2 * j
            cr = jnp.concatenate([bs_ref[pl.ds(n2, NBP, stride=PB), :],
                                  bs_ref[pl.ds(n2 + 1, NBP, stride=PB), :]], axis=1)
            ci = jnp.concatenate([bs_ref[pl.ds(N2 + n2, NBP, stride=PB), :],
                                  bs_ref[pl.ds(N2 + n2 + 1, NBP, stride=PB), :]], axis=1)
            c = jnp.concatenate([cr, kpad, ci, kpad], axis=0).astype(BF16)
            y = _dot(ga_ref[...], c)
            zs_ref[pl.ds(n2, NBP, stride=PZ), :] = y[:, :LANE]
            zs_ref[pl.ds(n2 + 1, NBP, stride=PZ), :] = y[:, LANE:]
            return carry

        lax.fori_loop(0, N2 // 2, body, 0, unroll=4)
        for n1 in range(NB):
            y_ref[0, n1 * N2:(n1 + 1) * N2, :] = zs_ref[n1 * PZ:n1 * PZ + N2, :]


def long_conv(z, kf, fa, fb, ga):
    blk = pl.BlockSpec((1, TP, LANE), lambda c, k: (c, 0, 0))
    return pl.pallas_call(
        _long_conv_kernel,
        grid=(NCB, NKC),
        in_specs=[blk,
                  pl.BlockSpec((1, KC, 2 * N2, LANE), lambda c, k: (c, k, 0, 0)),
                  pl.BlockSpec(fa.shape, lambda c, k: (0, 0)),
                  pl.BlockSpec(fb.shape, lambda c, k: (0, 0, 0), pipeline_mode=pl.Buffered(1)),
                  pl.BlockSpec(ga.shape, lambda c, k: (0, 0))],
        out_specs=blk,
        out_shape=jax.ShapeDtypeStruct((NCB, TP, LANE), F32),
        scratch_shapes=[pltpu.VMEM((NBP * PZ, LANE), F32),
                        pltpu.VMEM((N2 * PA, LANE), F32),
                        pltpu.VMEM((NBP * PB, LANE), F32)],
        compiler_params=_cparams(("parallel", "arbitrary")),
        name="long_conv",
    )(z, kf, fa, fb, ga)


@functools.lru_cache(maxsize=None)
def _dft_constants():
    k1 = np.arange(KH, dtype=np.int64)

    n1 = np.arange(NB, dtype=np.int64)
    ang = 2.0 * np.pi * ((k1[:, None] * n1[None, :]) % N1) / N1
    fa = np.zeros((PA + 8, LANE))
    fa[:KH, :NB] = np.cos(ang)
    fa[IM0:IM0 + KH, :NB] = -np.sin(ang)

    n2 = np.arange(N2, dtype=np.int64)
    k2 = np.arange(N2, dtype=np.int64)
    freq = k1[:, None, None] + N1 * k2[None, :, None]
    theta = 2.0 * np.pi * ((freq * n2[None, None, :]) % NFFT) / NFFT
    c, s = np.cos(theta), np.sin(theta)
    fb = np.concatenate([np.concatenate([c, s], axis=2), np.concatenate([-s, c], axis=2)], axis=1)

    herm = np.where((k1 == 0) | (k1 == N1 // 2), 1.0, 2.0) / NFFT
    phi = 2.0 * np.pi * ((n1[:, None] * k1[None, :]) % N1) / N1
    ga = np.zeros((NBP, 2 * LANE))
    ga[:NB, :KH] = np.cos(phi) * herm[None, :]
    ga[:NB, LANE:LANE + KH] = -np.sin(phi) * herm[None, :]
    return tuple(m.astype(np.float32) for m in (fa, fb, ga))


@functools.lru_cache(maxsize=None)
def _position_tables():
    pos = np.arange(TP, dtype=np.float64)
    inv = ROPE_THETA ** (-np.arange(0, QK_ROPE, 2, dtype=np.float64) / QK_ROPE)
    ang = pos[:, None] * inv[None, :]
    ang = np.concatenate([ang, ang], axis=-1)
    pad = np.zeros((TP, LANE - QK_ROPE))
    cosp = np.concatenate([np.cos(ang), pad], axis=-1)
    sinp = np.concatenate([np.sin(ang), pad], axis=-1)

    p = np.minimum(np.arange(TP), T - 1).astype(np.float64)
    tl = p / (T - 1)
    freqs = np.linspace(1e-4, FILT_BANDS - 1, FILT_BANDS)
    fang = (2.0 * np.pi * p / T)[None, :] * freqs[:, None]
    zt = np.zeros((FILT_ORDER, TP))
    zt[0] = tl
    zt[1:1 + FILT_BANDS] = np.cos(fang)
    zt[1 + FILT_BANDS:FILT_EMB] = -np.sin(fang)
    live = np.arange(TP) < T
    aux = np.zeros((TP, 8))
    aux[:, 0] = tl
    aux[:, 1] = live
    aux[:, 2] = live & (np.arange(TP) >= 1)
    max_decay = math.log(DECAY_TARGET) / FAST_DECAY_PCT
    min_decay = math.log(DECAY_TARGET) / SLOW_DECAY_PCT
    deltas = np.abs(np.linspace(min_decay, max_decay, HY_WIDTH))[None, :]
    return tuple(m.astype(np.float32) for m in (cosp, sinp, zt, aux, deltas))


def _rot_cols(w):
    half = QK_ROPE // 2
    return jnp.concatenate([-w[..., half:], w[..., :half]], axis=-1)


def _rope_gain_rows(g):
    half = QK_ROPE // 2
    gr = g[QK_NOPE:]
    pad = jnp.zeros((LANE - QK_ROPE,), F32)
    return jnp.stack([g[:QK_NOPE], jnp.concatenate([gr, pad]),
                      jnp.concatenate([gr[half:], gr[:half], pad])])


def kernel(x, meta_tokens, norm_mix_g, w_in, q_lat_g, kv_lat_g, w_uq, w_ukv, q_norm_g, k_norm_g, conv_w, conv_b, filt_w1, filt_b1, filt_freq1, filt_w2, filt_b2, filt_freq2, filt_w3, hy_skip, attn_out_g, hy_out_g, w_out, norm_ffn_g, w_gate, w_up, w_down):
    assert x.shape == (1, SEQ, D_MODEL)
    h = jnp.concatenate([meta_tokens.astype(F32), x[0], jnp.zeros((TP - T, D_MODEL), F32)], axis=0)

    cosp, sinp, zt, aux, deltas = (jnp.asarray(m) for m in _position_tables())
    fa, fb, ga = (jnp.asarray(m).astype(BF16) for m in _dft_constants())
    epad = FILT_ORDER - FILT_EMB
    col = lambda v: v[:, None]

    o2, o3 = Q_LORA + KV_LORA, Q_LORA + KV_LORA + QK_ROPE
    lane_pad = lambda a, lo, hi: jnp.pad(a, ((0, 0), (0, 0), (lo, hi)))
    w_in_big = (lane_pad(w_in[..., :o3], 0, P_COLS - o3)
                + lane_pad(_rot_cols(w_in[..., o2:o3]), o3, P_COLS - o3 - QK_ROPE)
                + lane_pad(w_in[..., o3:], P_ATT, 0)).astype(BF16)
    wq = w_uq.astype(BF16).reshape(DEPTH, Q_LORA, ATT_HEADS, QK_DIM)
    wq = jnp.concatenate([wq, _rot_cols(wq[..., QK_NOPE:])], axis=-1)
    wq = wq.reshape(DEPTH, Q_LORA, ATT_HEADS * HEAD_PAD)
    wkv = w_ukv.astype(BF16)
    w_out_b, w_gate_b, w_up_b, w_down_b = (w.astype(BF16) for w in (w_out, w_gate, w_up, w_down))

    for l in range(DEPTH):
        p = norm_matmul(h, norm_mix_g[l][None], w_in_big, l, 2048)
        q, k, v = mla_prep(p, cosp, sinp, q_lat_g[l][None], kv_lat_g[l][None], wq, wkv, l,
                           _rope_gain_rows(q_norm_g[l]), _rope_gain_rows(k_norm_g[l]))
        att = attention(q, k, v)

        z, u = hyena_pre(p, conv_w[l], conv_b[l][None])
        hf, hb = hyena_filter(
            zt, aux,
            jnp.pad(filt_w1[l].T, ((0, 0), (0, epad))), col(filt_b1[l]), col(filt_freq1[l]),
            filt_w2[l].T, col(filt_b2[l]), col(filt_freq2[l]),
            filt_w3[l].astype(BF16), deltas)
        kf = filter_spectrum(hf, hb, fa, fb)
        y = long_conv(z, kf, fa, fb, ga)

        h = mix_matmul_residual(att, y, z, u, hy_skip[l][None], attn_out_g[l][None], hy_out_g[l][None],
                                w_out_b, l, h)

        act = norm_swiglu(h, norm_ffn_g[l][None], w_gate_b, w_up_b, l, 512)
        h = matmul_residual(act, w_down_b, l, h)

    return h[N_META:T][None]
```

```python
import functools
import math

import numpy as np
import jax
import jax.numpy as jnp
from jax import lax
from jax.experimental import pallas as pl
from jax.experimental.pallas import tpu as pltpu

F32 = jnp.float32
BF16 = jnp.bfloat16

D_MODEL = 2048
SEQ = 8192
DEPTH = 4
N_META = 16
T = N_META + SEQ
V_DIM = 128
QK_NOPE = 128
QK_ROPE = 64
QK_DIM = QK_NOPE + QK_ROPE
ATT_HEADS = 8
ATT_WIDTH = ATT_HEADS * V_DIM
Q_LORA = 512
KV_LORA = 256
HY_WIDTH = 1024
FILT_EMB = 33
FILT_BANDS = 16
FILT_ORDER = 64
DECAY_TARGET = 1e-2
FAST_DECAY_PCT = 0.3
SLOW_DECAY_PCT = 1.5
D_FF = 5632
ROPE_THETA = 10000.0
EPS = 1e-6

LANE = 128
HEAD_PAD = 2 * LANE
VMEM_LIMIT = 56 * 1024 * 1024

TP = 8320
TM = 640
TM_FFN = 1664
TM_MIX = 320
TQ = 1040
TK = 2048
P_COLS = 4096
P_ATT = 1024
TH = 1024

N2 = 128
N1 = 130
NFFT = N1 * N2
KH = N1 // 2 + 1
KC = 22
NKC = KH // KC
NB = TP // N2
NBP = 72
NCB = HY_WIDTH // LANE
PZ = 136
PA = 136
IM0 = 68
PB = 264


def _cparams(sem, flags=None):
    return pltpu.CompilerParams(dimension_semantics=sem, vmem_limit_bytes=VMEM_LIMIT, flags=flags)


def _rms(x):
    return lax.rsqrt(jnp.mean(x * x, axis=-1, keepdims=True) + EPS)


def _dot(a, b):
    return jnp.dot(a, b, preferred_element_type=F32)


def _norm_mm_kernel(x_ref, g_ref, w_ref, o_ref):
    x = x_ref[...]
    o_ref[...] = _dot((x * _rms(x) * g_ref[...]).astype(BF16), w_ref[...])


def _layer_spec(l, k, tn):
    return pl.BlockSpec((None, k, tn), lambda i, j: (l, 0, j))


def norm_matmul(x, g, w, l, tn):
    m, k = x.shape
    n = w.shape[2]
    return pl.pallas_call(
        _norm_mm_kernel,
        grid=(n // tn, m // TM),
        in_specs=[pl.BlockSpec((TM, k), lambda j, i: (i, 0)),
                  pl.BlockSpec((1, k), lambda j, i: (0, 0)),
                  pl.BlockSpec((None, k, tn), lambda j, i: (l, 0, j))],
        out_specs=pl.BlockSpec((TM, tn), lambda j, i: (i, j)),
        out_shape=jax.ShapeDtypeStruct((m, n), F32),
        compiler_params=_cparams(("parallel", "parallel")),
        name="norm_matmul",
    )(x, g, w)


def _norm_swiglu_kernel(x_ref, g_ref, wg_ref, wu_ref, o_ref, xn_ref):
    @pl.when(pl.program_id(1) == 0)
    def _():
        x = x_ref[...]
        xn_ref[...] = (x * _rms(x) * g_ref[...]).astype(BF16)

    xn = xn_ref[...]
    a = _dot(xn, wg_ref[...])
    b = _dot(xn, wu_ref[...])
    o_ref[...] = (a * jax.nn.sigmoid(a) * b).astype(BF16)


def norm_swiglu(x, g, wg, wu, l, tn):
    m, k = x.shape
    n = wg.shape[2]
    return pl.pallas_call(
        _norm_swiglu_kernel,
        grid=(m // TM_FFN, n // tn),
        in_specs=[pl.BlockSpec((TM_FFN, k), lambda i, j: (i, 0)),
                  pl.BlockSpec((1, k), lambda i, j: (0, 0)),
                  _layer_spec(l, k, tn),
                  _layer_spec(l, k, tn)],
        out_specs=pl.BlockSpec((TM_FFN, tn), lambda i, j: (i, j)),
        out_shape=jax.ShapeDtypeStruct((m, n), BF16),
        scratch_shapes=[pltpu.VMEM((TM_FFN, k), BF16)],
        compiler_params=_cparams(("parallel", "arbitrary")),
        name="norm_swiglu",
    )(x, g, wg, wu)


def _mm_res_kernel(a_ref, w_ref, r_ref, o_ref):
    o_ref[...] = r_ref[...] + _dot(a_ref[...], w_ref[...])


def matmul_residual(a, w, l, res):
    m, k = a.shape
    n = w.shape[2]
    return pl.pallas_call(
        _mm_res_kernel,
        grid=(m // TM_MIX,),
        in_specs=[pl.BlockSpec((TM_MIX, k), lambda i: (i, 0)),
                  pl.BlockSpec((None, k, n), lambda i: (l, 0, 0), pipeline_mode=pl.Buffered(1)),
                  pl.BlockSpec((TM_MIX, n), lambda i: (i, 0))],
        out_specs=pl.BlockSpec((TM_MIX, n), lambda i: (i, 0)),
        out_shape=jax.ShapeDtypeStruct((m, n), F32),
        compiler_params=_cparams(("parallel",)),
        name="matmul_residual",
    )(a, w, res)


def _mix_mm_res_kernel(a_ref, y_ref, z_ref, u_ref, d_ref, ga_ref, gy_ref, w_ref, r_ref, o_ref):
    a = a_ref[...]
    cat = lambda ref: jnp.concatenate([ref[c] for c in range(NCB)], axis=-1)
    y = (cat(y_ref) + cat(z_ref) * d_ref[...]) * cat(u_ref)
    mix = jnp.concatenate([(a * _rms(a) * ga_ref[...]).astype(BF16),
                           (y * _rms(y) * gy_ref[...]).astype(BF16)], axis=-1)
    o_ref[...] = r_ref[...] + _dot(mix, w_ref[...])


def mix_matmul_residual(a, y, z, u, d, ga, gy, w, l, res):
    m = a.shape[0]
    k, n = w.shape[1:]
    blk = pl.BlockSpec((NCB, TM_MIX, LANE), lambda i: (0, i, 0))
    row = lambda width: pl.BlockSpec((1, width), lambda i: (0, 0))
    return pl.pallas_call(
        _mix_mm_res_kernel,
        grid=(m // TM_MIX,),
        in_specs=[pl.BlockSpec((TM_MIX, ATT_WIDTH), lambda i: (i, 0)), blk, blk, blk,
                  row(HY_WIDTH), row(ATT_WIDTH), row(HY_WIDTH),
                  pl.BlockSpec((None, k, n), lambda i: (l, 0, 0), pipeline_mode=pl.Buffered(1)),
                  pl.BlockSpec((TM_MIX, n), lambda i: (i, 0))],
        out_specs=pl.BlockSpec((TM_MIX, n), lambda i: (i, 0)),
        out_shape=jax.ShapeDtypeStruct((m, n), F32),
        compiler_params=_cparams(("parallel",)),
        name="mix_matmul_residual",
    )(a, y, z, u, d, ga, gy, w, res)


def _mla_prep_kernel(p_ref, cos_ref, sin_ref, glq_ref, glkv_ref, wq_ref, wkv_ref, gq_ref, gk_ref,
                     q_ref, k_ref, v_ref):
    p = p_ref[...]
    cq = p[:, :Q_LORA]
    ckv = p[:, Q_LORA:Q_LORA + KV_LORA]
    kr = p[:, Q_LORA + KV_LORA:Q_LORA + KV_LORA + LANE]
    qp = _dot((cq * _rms(cq) * glq_ref[...]).astype(BF16), wq_ref[...])
    kvp = _dot((ckv * _rms(ckv) * glkv_ref[...]).astype(BF16), wkv_ref[...])
    cosp = cos_ref[...]
    sinp = sin_ref[...]
    rope_lane = lax.broadcasted_iota(jnp.int32, (1, LANE), 1) < QK_ROPE
    gq = gq_ref[...]
    gk = gk_ref[...]
    q_cos, q_sin = cosp * gq[1:2], sinp * gq[2:3]
    k_cos, k_sin = cosp * gk[1:2], sinp * gk[2:3]
    kr_ss = jnp.sum(jnp.where(rope_lane, kr * kr, 0.0), axis=-1, keepdims=True)
    kr_rot = kr * k_cos + pltpu.roll(kr, QK_ROPE, 1) * k_sin
    q_scale = QK_DIM ** -0.5 * math.log2(math.e)
    for h in range(ATT_HEADS):
        qn = qp[:, h * HEAD_PAD:h * HEAD_PAD + LANE]
        qa = qp[:, h * HEAD_PAD + LANE:(h + 1) * HEAD_PAD]
        ss = (jnp.sum(qn * qn, axis=-1, keepdims=True)
              + jnp.sum(jnp.where(rope_lane, qa * qa, 0.0), axis=-1, keepdims=True))
        rq = lax.rsqrt(ss * (1.0 / QK_DIM) + EPS) * q_scale
        q_ref[h, :, :LANE] = (qn * rq * gq[0:1]).astype(BF16)
        q_ref[h, :, LANE:] = ((qa * q_cos + pltpu.roll(qa, QK_ROPE, 1) * q_sin) * rq).astype(BF16)
        kn = kvp[:, h * HEAD_PAD:h * HEAD_PAD + LANE]
        rk = lax.rsqrt((jnp.sum(kn * kn, axis=-1, keepdims=True) + kr_ss) * (1.0 / QK_DIM) + EPS)
        k_ref[h, :, :LANE] = (kn * rk * gk[0:1]).astype(BF16)
        k_ref[h, :, LANE:] = (kr_rot * rk).astype(BF16)
        v_ref[h] = kvp[:, h * HEAD_PAD + LANE:(h + 1) * HEAD_PAD].astype(BF16)


def mla_prep(p, cosp, sinp, glq, glkv, wq, wkv, l, gq, gk):
    const = lambda i: (0, 0)
    return pl.pallas_call(
        _mla_prep_kernel,
        grid=(TP // TM,),
        in_specs=[pl.BlockSpec((TM, P_ATT), lambda i: (i, 0)),
                  pl.BlockSpec((TM, LANE), lambda i: (i, 0)),
                  pl.BlockSpec((TM, LANE), lambda i: (i, 0)),
                  pl.BlockSpec((1, Q_LORA), const),
                  pl.BlockSpec((1, KV_LORA), const),
                  pl.BlockSpec((None, Q_LORA, ATT_HEADS * HEAD_PAD), lambda i: (l, 0, 0)),
                  pl.BlockSpec((None, KV_LORA, ATT_HEADS * HEAD_PAD), lambda i: (l, 0, 0)),
                  pl.BlockSpec((3, LANE), const),
                  pl.BlockSpec((3, LANE), const)],
        out_specs=[pl.BlockSpec((ATT_HEADS, TM, HEAD_PAD), lambda i: (0, i, 0)),
                   pl.BlockSpec((ATT_HEADS, TM, HEAD_PAD), lambda i: (0, i, 0)),
                   pl.BlockSpec((ATT_HEADS, TM, V_DIM), lambda i: (0, i, 0))],
        out_shape=[jax.ShapeDtypeStruct((ATT_HEADS, TP, HEAD_PAD), BF16),
                   jax.ShapeDtypeStruct((ATT_HEADS, TP, HEAD_PAD), BF16),
                   jax.ShapeDtypeStruct((ATT_HEADS, TP, V_DIM), BF16)],
        compiler_params=_cparams(("parallel",)),
        name="mla_prep",
    )(p, cosp, sinp, glq, glkv, wq, wkv, gq, gk)


def _attn_kernel(q_ref, k_ref, v_ref, o_ref):
    q = q_ref[0]
    m = jnp.full((TQ, 1), -jnp.inf, F32)
    l = jnp.zeros((TQ, 1), F32)
    acc = jnp.zeros((TQ, V_DIM), F32)
    for k0 in range(0, TP, TK):
        k1 = min(k0 + TK, TP)
        s = lax.dot_general(q, k_ref[0, k0:k1, :], (((1,), (1,)), ((), ())), preferred_element_type=F32)
        if k1 > T:
            col = k0 + lax.broadcasted_iota(jnp.int32, (1, k1 - k0), 1)
            s = jnp.where(col < T, s, -jnp.inf)
        m_new = jnp.maximum(m, jnp.max(s, axis=-1, keepdims=True))
        alpha = jnp.exp2(m - m_new)
        pr = jnp.exp2(s - m_new)
        l = alpha * l + jnp.sum(pr, axis=-1, keepdims=True)
        acc = alpha * acc + _dot(pr.astype(BF16), v_ref[0, k0:k1, :])
        m = m_new
    o_ref[...] = acc / l


def attention(q, k, v):
    return pl.pallas_call(
        _attn_kernel,
        grid=(ATT_HEADS, TP // TQ),
        in_specs=[pl.BlockSpec((1, TQ, HEAD_PAD), lambda h, i: (h, i, 0)),
                  pl.BlockSpec((1, TP, HEAD_PAD), lambda h, i: (h, 0, 0)),
                  pl.BlockSpec((1, TP, V_DIM), lambda h, i: (h, 0, 0))],
        out_specs=pl.BlockSpec((TQ, V_DIM), lambda h, i: (i, h)),
        out_shape=jax.ShapeDtypeStruct((TP, ATT_WIDTH), F32),
        compiler_params=_cparams(("parallel", "parallel")),
        name="attention",
    )(q, k, v)


def _short_conv(x, prev_row, next_row, w, b, t0):
    rows = x.shape[0]
    r = lax.broadcasted_iota(jnp.int32, (rows, 1), 0)
    t = t0 + r
    xm = jnp.where(r == 0, prev_row, pltpu.roll(x, 1, 0))
    xm = jnp.where(t == 0, 0.0, xm)
    xp = jnp.where(r == rows - 1, next_row, pltpu.roll(x, rows - 1, 0))
    xp = jnp.where(t >= T - 1, 0.0, xp)
    return xm * w[0:1] + x * w[1:2] + xp * w[2:3] + b


def _hy_pre_kernel(x0_ref, x1_ref, v_ref, x0p_ref, x1p_ref, vp_ref, x0n_ref, x1n_ref, vn_ref,
                   w0_ref, w1_ref, wv_ref, b0_ref, b1_ref, bv_ref, z_ref, u_ref):
    t0 = pl.program_id(0) * TH
    valid = (t0 + lax.broadcasted_iota(jnp.int32, (TH, 1), 0)) < T
    u0 = _short_conv(x0_ref[...], x0p_ref[7:8], x0n_ref[0:1], w0_ref[...], b0_ref[...], t0)
    u1 = _short_conv(x1_ref[...], x1p_ref[7:8], x1n_ref[0:1], w1_ref[...], b1_ref[...], t0)
    uv = _short_conv(v_ref[...], vp_ref[7:8], vn_ref[0:1], wv_ref[...], bv_ref[...], t0)
    z = jnp.where(valid, uv * u1, 0.0)
    u = jnp.where(valid, u0, 0.0)
    for c in range(z_ref.shape[0]):
        z_ref[c] = z[:, c * LANE:(c + 1) * LANE]
        u_ref[c] = u[:, c * LANE:(c + 1) * LANE]


def hyena_pre(p, conv_w, conv_b):
    cw = 512
    ncw = HY_WIDTH // cw
    off = P_ATT // cw
    halo = TH // 8
    last8 = TP // 8 - 1

    def main(s):
        return pl.BlockSpec((TH, cw), lambda i, j: (i, off + s * ncw + j))

    def prev(s):
        return pl.BlockSpec((8, cw), lambda i, j: (jnp.maximum(i * halo - 1, 0), off + s * ncw + j))

    def nxt(s):
        return pl.BlockSpec((8, cw), lambda i, j: (jnp.minimum((i + 1) * halo, last8), off + s * ncw + j))

    def par(s, rows):
        return pl.BlockSpec((rows, cw), lambda i, j: (0, s * ncw + j))

    out_spec = pl.BlockSpec((cw // LANE, TH, LANE), lambda i, j: (j, i, 0))
    out_sds = jax.ShapeDtypeStruct((NCB, TP, LANE), F32)
    return pl.pallas_call(
        _hy_pre_kernel,
        grid=(pl.cdiv(TP, TH), ncw),
        in_specs=[main(0), main(1), main(2), prev(0), prev(1), prev(2), nxt(0), nxt(1), nxt(2),
                  par(0, 3), par(1, 3), par(2, 3), par(0, 1), par(1, 1), par(2, 1)],
        out_specs=[out_spec, out_spec],
        out_shape=[out_sds, out_sds],
        compiler_params=_cparams(("parallel", "parallel")),
        name="hyena_pre",
    )(p, p, p, p, p, p, p, p, p, conv_w, conv_w, conv_w, conv_b, conv_b, conv_b)


def _filter_kernel(zt_ref, aux_ref, w1_ref, b1_ref, f1_ref, w2_ref, b2_ref, f2_ref, w3_ref, dl_ref,
                   hf_ref, hb_ref):
    hi = lax.Precision.HIGHEST
    h = jnp.sin(f1_ref[...] * (jnp.dot(w1_ref[...], zt_ref[...], precision=hi,
                                       preferred_element_type=F32) + b1_ref[...]))
    h = jnp.sin(f2_ref[...] * (jnp.dot(w2_ref[...], h, precision=hi,
                                       preferred_element_type=F32) + b2_ref[...]))
    hh = lax.dot_general(h.astype(BF16), w3_ref[...], (((0,), (0,)), ((), ())),
                         preferred_element_type=F32)
    aux = aux_ref[...]
    decay = jnp.exp(-aux[:, 0:1] * dl_ref[...])
    hf = hh[:, :HY_WIDTH] * (decay * aux[:, 1:2])
    hb = hh[:, HY_WIDTH:] * (decay * aux[:, 2:3])
    for c in range(NCB):
        hf_ref[c] = hf[:, c * LANE:(c + 1) * LANE].astype(BF16)
        hb_ref[c] = hb[:, c * LANE:(c + 1) * LANE].astype(BF16)


def hyena_filter(zt, aux, w1t, b1, f1, w2t, b2, f2, w3, deltas):
    const = lambda i: (0, 0)
    out_spec = pl.BlockSpec((NCB, TM, LANE), lambda i: (0, i, 0))
    out_sds = jax.ShapeDtypeStruct((NCB, TP, LANE), BF16)
    col = pl.BlockSpec((FILT_ORDER, 1), const)
    return pl.pallas_call(
        _filter_kernel,
        grid=(TP // TM,),
        in_specs=[pl.BlockSpec((FILT_ORDER, TM), lambda i: (0, i)),
                  pl.BlockSpec((TM, 8), lambda i: (i, 0)),
                  pl.BlockSpec((FILT_ORDER, FILT_ORDER), const), col, col,
                  pl.BlockSpec((FILT_ORDER, FILT_ORDER), const), col, col,
                  pl.BlockSpec((FILT_ORDER, 2 * HY_WIDTH), const),
                  pl.BlockSpec((1, HY_WIDTH), const)],
        out_specs=[out_spec, out_spec],
        out_shape=[out_sds, out_sds],
        compiler_params=_cparams(("parallel",)),
        name="hyena_filter",
    )(zt, aux, w1t, b1, f1, w2t, b2, f2, w3, deltas)


def _stage_a(x_ref, fa_ref, zs_ref, as_ref):
    for n1 in range(NB):
        zs_ref[n1 * PZ:n1 * PZ + N2, :] = x_ref[0, n1 * N2:(n1 + 1) * N2, :].astype(F32)
    zs_ref[NB * PZ:, :] = jnp.zeros(((NBP - NB) * PZ, LANE), F32)
    kpad = jnp.zeros((fa_ref.shape[1] - NBP, 2 * LANE), F32)

    def body(j, carry):
        n2 = 2 * j
        x = jnp.concatenate([zs_ref[pl.ds(n2, NBP, stride=PZ), :],
                             zs_ref[pl.ds(n2 + 1, NBP, stride=PZ), :]], axis=1)
        r = _dot(fa_ref[...], jnp.concatenate([x, kpad], axis=0).astype(BF16))
        off = pl.multiple_of(n2 * PA, 8)
        as_ref[pl.ds(off, PA), :] = r[:PA, :LANE]
        as_ref[pl.ds(off + PA, PA), :] = r[:PA, LANE:]
        return carry

    lax.fori_loop(0, N2 // 2, body, 0, unroll=4)


def _stage_b_fwd(as_ref, fb, k1):
    ar = as_ref[pl.ds(k1, N2, stride=PA), :]
    ai = as_ref[pl.ds(IM0 + k1, N2, stride=PA), :]
    return _dot(fb, jnp.concatenate([ar, ai], axis=0).astype(BF16))


def _filter_spectrum_kernel(hf_ref, hb_ref, fa_ref, fb_ref, o_ref, zs_ref, af_ref, ab_ref):
    kc = pl.program_id(1)

    @pl.when(kc == 0)
    def _():
        _stage_a(hf_ref, fa_ref, zs_ref, af_ref)
        _stage_a(hb_ref, fa_ref, zs_ref, ab_ref)

    for k in range(KC):
        k1 = kc * KC + k
        sf = _stage_b_fwd(af_ref, fb_ref[k1], k1)
        sb = _stage_b_fwd(ab_ref, fb_ref[k1], k1)
        o_ref[0, k, :N2, :] = (sf[:N2] + sb[:N2]).astype(BF16)
        o_ref[0, k, N2:, :] = (sf[N2:] - sb[N2:]).astype(BF16)


def filter_spectrum(hf, hb, fa, fb):
    blk = pl.BlockSpec((1, TP, LANE), lambda c, k: (c, 0, 0))
    return pl.pallas_call(
        _filter_spectrum_kernel,
        grid=(NCB, NKC),
        in_specs=[blk, blk,
                  pl.BlockSpec(fa.shape, lambda c, k: (0, 0)),
                  pl.BlockSpec(fb.shape, lambda c, k: (0, 0, 0), pipeline_mode=pl.Buffered(1))],
        out_specs=pl.BlockSpec((1, KC, 2 * N2, LANE), lambda c, k: (c, k, 0, 0)),
        out_shape=jax.ShapeDtypeStruct((NCB, KH, 2 * N2, LANE), BF16),
        scratch_shapes=[pltpu.VMEM((NBP * PZ, LANE), F32),
                        pltpu.VMEM((N2 * PA, LANE), F32),
                        pltpu.VMEM((N2 * PA, LANE), F32)],
        compiler_params=_cparams(("parallel", "arbitrary")),
        name="filter_spectrum",
    )(hf, hb, fa, fb)


def _long_conv_kernel(z_ref, kf_ref, fa_ref, fb_ref, ga_ref, y_ref, zs_ref, as_ref, bs_ref):
    kc = pl.program_id(1)

    @pl.when(kc == 0)
    def _():
        _stage_a(z_ref, fa_ref, zs_ref, as_ref)
        bs_ref[KH * PB:, :] = jnp.zeros(((NBP - KH) * PB, LANE), F32)

    for k in range(KC):
        k1 = kc * KC + k
        xs = _stage_b_fwd(as_ref, fb_ref[k1], k1)
        xr, xi = xs[:N2], xs[N2:]
        fr, fi = kf_ref[0, k, :N2, :].astype(F32), kf_ref[0, k, N2:, :].astype(F32)
        ys = jnp.concatenate([xr * fr - xi * fi, xr * fi + xi * fr], axis=0).astype(BF16)
        bs_ref[pl.ds(pl.multiple_of(k1 * PB, 8), 2 * N2), :] = lax.dot_general(
            fb_ref[k1], ys, (((0,), (0,)), ((), ())), preferred_element_type=F32)

    @pl.when(kc == NKC - 1)
    def _():
        kpad = jnp.zeros((LANE - NBP, 2 * LANE), F32)

        def body(j, carry):
            n2 = 2 * j
            cr = jnp.concatenate([bs_ref[pl.ds(n2, NBP, stride=PB), :],
                                  bs_ref[pl.ds(n2 + 1, NBP, stride=PB), :]], axis=1)
            ci = jnp.concatenate([bs_ref[pl.ds(N2 + n2, NBP, stride=PB), :],
                                  bs_ref[pl.ds(N2 + n2 + 1, NBP, stride=PB), :]], axis=1)
            c = jnp.concatenate([cr, kpad, ci, kpad], axis=0).astype(BF16)
            y = _dot(ga_ref[...], c)
            zs_ref[pl.ds(n2, NBP, stride=PZ), :] = y[:, :LANE]
            zs_ref[pl.ds(n2 + 1, NBP, stride=PZ), :] = y[:, LANE:]
            return carry

        lax.fori_loop(0, N2 // 2, body, 0, unroll=4)
        for n1 in range(NB):
            y_ref[0, n1 * N2:(n1 + 1) * N2, :] = zs_ref[n1 * PZ:n1 * PZ + N2, :]


def long_conv(z, kf, fa, fb, ga):
    blk = pl.BlockSpec((1, TP, LANE), lambda c, k: (c, 0, 0))
    return pl.pallas_call(
        _long_conv_kernel,
        grid=(NCB, NKC),
        in_specs=[blk,
                  pl.BlockSpec((1, KC, 2 * N2, LANE), lambda c, k: (c, k, 0, 0)),
                  pl.BlockSpec(fa.shape, lambda c, k: (0, 0)),
                  pl.BlockSpec(fb.shape, lambda c, k: (0, 0, 0), pipeline_mode=pl.Buffered(1)),
                  pl.BlockSpec(ga.shape, lambda c, k: (0, 0))],
        out_specs=blk,
        out_shape=jax.ShapeDtypeStruct((NCB, TP, LANE), F32),
        scratch_shapes=[pltpu.VMEM((NBP * PZ, LANE), F32),
                        pltpu.VMEM((N2 * PA, LANE), F32),
                        pltpu.VMEM((NBP * PB, LANE), F32)],
        compiler_params=_cparams(("parallel", "arbitrary")),
        name="long_conv",
    )(z, kf, fa, fb, ga)


@functools.lru_cache(maxsize=None)
def _dft_constants():
    k1 = np.arange(KH, dtype=np.int64)

    n1 = np.arange(NB, dtype=np.int64)
    ang = 2.0 * np.pi * ((k1[:, None] * n1[None, :]) % N1) / N1
    fa = np.zeros((PA + 8, LANE))
    fa[:KH, :NB] = np.cos(ang)
    fa[IM0:IM0 + KH, :NB] = -np.sin(ang)

    n2 = np.arange(N2, dtype=np.int64)
    k2 = np.arange(N2, dtype=np.int64)
    freq = k1[:, None, None] + N1 * k2[None, :, None]
    theta = 2.0 * np.pi * ((freq * n2[None, None, :]) % NFFT) / NFFT
    c, s = np.cos(theta), np.sin(theta)
    fb = np.concatenate([np.concatenate([c, s], axis=2), np.concatenate([-s, c], axis=2)], axis=1)

    herm = np.where((k1 == 0) | (k1 == N1 // 2), 1.0, 2.0) / NFFT
    phi = 2.0 * np.pi * ((n1[:, None] * k1[None, :]) % N1) / N1
    ga = np.zeros((NBP, 2 * LANE))
    ga[:NB, :KH] = np.cos(phi) * herm[None, :]
    ga[:NB, LANE:LANE + KH] = -np.sin(phi) * herm[None, :]
    return tuple(m.astype(np.float32) for m in (fa, fb, ga))


@functools.lru_cache(maxsize=None)
def _position_tables():
    pos = np.arange(TP, dtype=np.float64)
    inv = ROPE_THETA ** (-np.arange(0, QK_ROPE, 2, dtype=np.float64) / QK_ROPE)
    ang = pos[:, None] * inv[None, :]
    ang = np.concatenate([ang, ang], axis=-1)
    pad = np.zeros((TP, LANE - QK_ROPE))
    cosp = np.concatenate([np.cos(ang), pad], axis=-1)
    sinp = np.concatenate([np.sin(ang), pad], axis=-1)

    p = np.minimum(np.arange(TP), T - 1).astype(np.float64)
    tl = p / (T - 1)
    freqs = np.linspace(1e-4, FILT_BANDS - 1, FILT_BANDS)
    fang = (2.0 * np.pi * p / T)[None, :] * freqs[:, None]
    zt = np.zeros((FILT_ORDER, TP))
    zt[0] = tl
    zt[1:1 + FILT_BANDS] = np.cos(fang)
    zt[1 + FILT_BANDS:FILT_EMB] = -np.sin(fang)
    live = np.arange(TP) < T
    aux = np.zeros((TP, 8))
    aux[:, 0] = tl
    aux[:, 1] = live
    aux[:, 2] = live & (np.arange(TP) >= 1)
    max_decay = math.log(DECAY_TARGET) / FAST_DECAY_PCT
    min_decay = math.log(DECAY_TARGET) / SLOW_DECAY_PCT
    deltas = np.abs(np.linspace(min_decay, max_decay, HY_WIDTH))[None, :]
    return tuple(m.astype(np.float32) for m in (cosp, sinp, zt, aux, deltas))


def _rot_cols(w):
    half = QK_ROPE // 2
    return jnp.concatenate([-w[..., half:], w[..., :half]], axis=-1)


def _rope_gain_rows(g):
    half = QK_ROPE // 2
    gr = g[QK_NOPE:]
    pad = jnp.zeros((LANE - QK_ROPE,), F32)
    return jnp.stack([g[:QK_NOPE], jnp.concatenate([gr, pad]),
                      jnp.concatenate([gr[half:], gr[:half], pad])])


def kernel(x, meta_tokens, norm_mix_g, w_in, q_lat_g, kv_lat_g, w_uq, w_ukv, q_norm_g, k_norm_g, conv_w, conv_b, filt_w1, filt_b1, filt_freq1, filt_w2, filt_b2, filt_freq2, filt_w3, hy_skip, attn_out_g, hy_out_g, w_out, norm_ffn_g, w_gate, w_up, w_down):
    assert x.shape == (1, SEQ, D_MODEL)
    h = jnp.concatenate([meta_tokens.astype(F32), x[0], jnp.zeros((TP - T, D_MODEL), F32)], axis=0)

    cosp, sinp, zt, aux, deltas = (jnp.asarray(m) for m in _position_tables())
    fa, fb, ga = (jnp.asarray(m).astype(BF16) for m in _dft_constants())
    epad = FILT_ORDER - FILT_EMB
    col = lambda v: v[:, None]

    o2, o3 = Q_LORA + KV_LORA, Q_LORA + KV_LORA + QK_ROPE
    lane_pad = lambda a, lo, hi: jnp.pad(a, ((0, 0), (0, 0), (lo, hi)))
    w_in_big = (lane_pad(w_in[..., :o3], 0, P_COLS - o3)
                + lane_pad(_rot_cols(w_in[..., o2:o3]), o3, P_COLS - o3 - QK_ROPE)
                + lane_pad(w_in[..., o3:], P_ATT, 0)).astype(BF16)
    wq = w_uq.astype(BF16).reshape(DEPTH, Q_LORA, ATT_HEADS, QK_DIM)
    wq = jnp.concatenate([wq, _rot_cols(wq[..., QK_NOPE:])], axis=-1)
    wq = wq.reshape(DEPTH, Q_LORA, ATT_HEADS * HEAD_PAD)
    wkv = w_ukv.astype(BF16)
    w_out_b, w_gate_b, w_up_b, w_down_b = (w.astype(BF16) for w in (w_out, w_gate, w_up, w_down))

    for l in range(DEPTH):
        p = norm_matmul(h, norm_mix_g[l][None], w_in_big, l, 2048)
        q, k, v = mla_prep(p, cosp, sinp, q_lat_g[l][None], kv_lat_g[l][None], wq, wkv, l,
                           _rope_gain_rows(q_norm_g[l]), _rope_gain_rows(k_norm_g[l]))
        att = attention(q, k, v)

        z, u = hyena_pre(p, conv_w[l], conv_b[l][None])
        hf, hb = hyena_filter(
            zt, aux,
            jnp.pad(filt_w1[l].T, ((0, 0), (0, epad))), col(filt_b1[l]), col(filt_freq1[l]),
            filt_w2[l].T, col(filt_b2[l]), col(filt_freq2[l]),
            filt_w3[l].astype(BF16), deltas)
        kf = filter_spectrum(hf, hb, fa, fb)
        y = long_conv(z, kf, fa, fb, ga)

        h = mix_matmul_residual(att, y, z, u, hy_skip[l][None], attn_out_g[l][None], hy_out_g[l][None],
                                w_out_b, l, h)

        act = norm_swiglu(h, norm_ffn_g[l][None], w_gate_b, w_up_b, l, 512)
        h = matmul_residual(act, w_down_b, l, h)

    return h[N_META:T][None]
```

```python
import functools
import math

import numpy as np
import jax
import jax.numpy as jnp
from jax import lax
from jax.experimental import pallas as pl
from jax.experimental.pallas import tpu as pltpu

F32 = jnp.float32
BF16 = jnp.bfloat16

D_MODEL = 2048
SEQ = 8192
DEPTH = 4
N_META = 16
T = N_META + SEQ
V_DIM = 128
QK_NOPE = 128
QK_ROPE = 64
QK_DIM = QK_NOPE + QK_ROPE
ATT_HEADS = 8
ATT_WIDTH = ATT_HEADS * V_DIM
Q_LORA = 512
KV_LORA = 256
HY_WIDTH = 1024
FILT_EMB = 33
FILT_BANDS = 16
FILT_ORDER = 64
DECAY_TARGET = 1e-2
FAST_DECAY_PCT = 0.3
SLOW_DECAY_PCT = 1.5
D_FF = 5632
ROPE_THETA = 10000.0
EPS = 1e-6

LANE = 128
HEAD_PAD = 2 * LANE
VMEM_LIMIT = 56 * 1024 * 1024

TP = 8320
TM = 640
TM_FFN = 1664
TM_MIX = 320
TQ = 1040
TK = 2048
P_COLS = 4096
P_ATT = 1024
TH = 1024

N2 = 128
N1 = 130
NFFT = N1 * N2
KH = N1 // 2 + 1
KC = 33
NKC = KH // KC
NB = TP // N2
NBP = 72
NCB = HY_WIDTH // LANE
PZ = 136
PA = 136
IM0 = 68
PB = 264


def _cparams(sem, flags=None):
    return pltpu.CompilerParams(dimension_semantics=sem, vmem_limit_bytes=VMEM_LIMIT, flags=flags)


def _rms(x):
    return lax.rsqrt(jnp.mean(x * x, axis=-1, keepdims=True) + EPS)


def _dot(a, b):
    return jnp.dot(a, b, preferred_element_type=F32)


def _norm_mm_kernel(x_ref, g_ref, w_ref, o_ref):
    x = x_ref[...]
    o_ref[...] = _dot((x * _rms(x) * g_ref[...]).astype(BF16), w_ref[...])


def _layer_spec(l, k, tn):
    return pl.BlockSpec((None, k, tn), lambda i, j: (l, 0, j))


def norm_matmul(x, g, w, l, tn):
    m, k = x.shape
    n = w.shape[2]
    return pl.pallas_call(
        _norm_mm_kernel,
        grid=(n // tn, m // TM),
        in_specs=[pl.BlockSpec((TM, k), lambda j, i: (i, 0)),
                  pl.BlockSpec((1, k), lambda j, i: (0, 0)),
                  pl.BlockSpec((None, k, tn), lambda j, i: (l, 0, j))],
        out_specs=pl.BlockSpec((TM, tn), lambda j, i: (i, j)),
        out_shape=jax.ShapeDtypeStruct((m, n), F32),
        compiler_params=_cparams(("parallel", "parallel")),
        name="norm_matmul",
    )(x, g, w)


def _norm_swiglu_kernel(x_ref, g_ref, wg_ref, wu_ref, o_ref, xn_ref):
    @pl.when(pl.program_id(1) == 0)
    def _():
        x = x_ref[...]
        xn_ref[...] = (x * _rms(x) * g_ref[...]).astype(BF16)

    xn = xn_ref[...]
    a = _dot(xn, wg_ref[...])
    b = _dot(xn, wu_ref[...])
    o_ref[...] = (a * jax.nn.sigmoid(a) * b).astype(BF16)


def norm_swiglu(x, g, wg, wu, l, tn):
    m, k = x.shape
    n = wg.shape[2]
    return pl.pallas_call(
        _norm_swiglu_kernel,
        grid=(m // TM_FFN, n // tn),
        in_specs=[pl.BlockSpec((TM_FFN, k), lambda i, j: (i, 0)),
                  pl.BlockSpec((1, k), lambda i, j: (0, 0)),
                  _layer_spec(l, k, tn),
                  _layer_spec(l, k, tn)],
        out_specs=pl.BlockSpec((TM_FFN, tn), lambda i, j: (i, j)),
        out_shape=jax.ShapeDtypeStruct((m, n), BF16),
        scratch_shapes=[pltpu.VMEM((TM_FFN, k), BF16)],
        compiler_params=_cparams(("parallel", "arbitrary")),
        name="norm_swiglu",
    )(x, g, wg, wu)


def _mm_res_kernel(a_ref, w_ref, r_ref, o_ref):
    o_ref[...] = r_ref[...] + _dot(a_ref[...], w_ref[...])


def matmul_residual(a, w, l, res):
    m, k = a.shape
    n = w.shape[2]
    return pl.pallas_call(
        _mm_res_kernel,
        grid=(m // TM_MIX,),
        in_specs=[pl.BlockSpec((TM_MIX, k), lambda i: (i, 0)),
                  pl.BlockSpec((None, k, n), lambda i: (l, 0, 0), pipeline_mode=pl.Buffered(1)),
                  pl.BlockSpec((TM_MIX, n), lambda i: (i, 0))],
        out_specs=pl.BlockSpec((TM_MIX, n), lambda i: (i, 0)),
        out_shape=jax.ShapeDtypeStruct((m, n), F32),
        compiler_params=_cparams(("parallel",)),
        name="matmul_residual",
    )(a, w, res)


def _mix_mm_res_kernel(a_ref, y_ref, z_ref, u_ref, d_ref, ga_ref, gy_ref, w_ref, r_ref, o_ref):
    a = a_ref[...]
    cat = lambda ref: jnp.concatenate([ref[c] for c in range(NCB)], axis=-1)
    y = (cat(y_ref) + cat(z_ref) * d_ref[...]) * cat(u_ref)
    mix = jnp.concatenate([(a * _rms(a) * ga_ref[...]).astype(BF16),
                           (y * _rms(y) * gy_ref[...]).astype(BF16)], axis=-1)
    o_ref[...] = r_ref[...] + _dot(mix, w_ref[...])


def mix_matmul_residual(a, y, z, u, d, ga, gy, w, l, res):
    m = a.shape[0]
    k, n = w.shape[1:]
    blk = pl.BlockSpec((NCB, TM_MIX, LANE), lambda i: (0, i, 0))
    row = lambda width: pl.BlockSpec((1, width), lambda i: (0, 0))
    return pl.pallas_call(
        _mix_mm_res_kernel,
        grid=(m // TM_MIX,),
        in_specs=[pl.BlockSpec((TM_MIX, ATT_WIDTH), lambda i: (i, 0)), blk, blk, blk,
                  row(HY_WIDTH), row(ATT_WIDTH), row(HY_WIDTH),
                  pl.BlockSpec((None, k, n), lambda i: (l, 0, 0), pipeline_mode=pl.Buffered(1)),
                  pl.BlockSpec((TM_MIX, n), lambda i: (i, 0))],
        out_specs=pl.BlockSpec((TM_MIX, n), lambda i: (i, 0)),
        out_shape=jax.ShapeDtypeStruct((m, n), F32),
        compiler_params=_cparams(("parallel",)),
        name="mix_matmul_residual",
    )(a, y, z, u, d, ga, gy, w, res)


def _mla_prep_kernel(p_ref, cos_ref, sin_ref, glq_ref, glkv_ref, wq_ref, wkv_ref, gq_ref, gk_ref,
                     q_ref, k_ref, v_ref):
    p = p_ref[...]
    cq = p[:, :Q_LORA]
    ckv = p[:, Q_LORA:Q_LORA + KV_LORA]
    kr = p[:, Q_LORA + KV_LORA:Q_LORA + KV_LORA + LANE]
    qp = _dot((cq * _rms(cq) * glq_ref[...]).astype(BF16), wq_ref[...])
    kvp = _dot((ckv * _rms(ckv) * glkv_ref[...]).astype(BF16), wkv_ref[...])
    cosp = cos_ref[...]
    sinp = sin_ref[...]
    rope_lane = lax.broadcasted_iota(jnp.int32, (1, LANE), 1) < QK_ROPE
    gq = gq_ref[...]
    gk = gk_ref[...]
    q_cos, q_sin = cosp * gq[1:2], sinp * gq[2:3]
    k_cos, k_sin = cosp * gk[1:2], sinp * gk[2:3]
    kr_ss = jnp.sum(jnp.where(rope_lane, kr * kr, 0.0), axis=-1, keepdims=True)
    kr_rot = kr * k_cos + pltpu.roll(kr, QK_ROPE, 1) * k_sin
    q_scale = QK_DIM ** -0.5 * math.log2(math.e)
    for h in range(ATT_HEADS):
        qn = qp[:, h * HEAD_PAD:h * HEAD_PAD + LANE]
        qa = qp[:, h * HEAD_PAD + LANE:(h + 1) * HEAD_PAD]
        ss = (jnp.sum(qn * qn, axis=-1, keepdims=True)
              + jnp.sum(jnp.where(rope_lane, qa * qa, 0.0), axis=-1, keepdims=True))
        rq = lax.rsqrt(ss * (1.0 / QK_DIM) + EPS) * q_scale
        q_ref[h, :, :LANE] = (qn * rq * gq[0:1]).astype(BF16)
        q_ref[h, :, LANE:] = ((qa * q_cos + pltpu.roll(qa, QK_ROPE, 1) * q_sin) * rq).astype(BF16)
        kn = kvp[:, h * HEAD_PAD:h * HEAD_PAD + LANE]
        rk = lax.rsqrt((jnp.sum(kn * kn, axis=-1, keepdims=True) + kr_ss) * (1.0 / QK_DIM) + EPS)
        k_ref[h, :, :LANE] = (kn * rk * gk[0:1]).astype(BF16)
        k_ref[h, :, LANE:] = (kr_rot * rk).astype(BF16)
        v_ref[h] = kvp[:, h * HEAD_PAD + LANE:(h + 1) * HEAD_PAD].astype(BF16)


def mla_prep(p, cosp, sinp, glq, glkv, wq, wkv, l, gq, gk):
    const = lambda i: (0, 0)
    return pl.pallas_call(
        _mla_prep_kernel,
        grid=(TP // TM,),
        in_specs=[pl.BlockSpec((TM, P_ATT), lambda i: (i, 0)),
                  pl.BlockSpec((TM, LANE), lambda i: (i, 0)),
                  pl.BlockSpec((TM, LANE), lambda i: (i, 0)),
                  pl.BlockSpec((1, Q_LORA), const),
                  pl.BlockSpec((1, KV_LORA), const),
                  pl.BlockSpec((None, Q_LORA, ATT_HEADS * HEAD_PAD), lambda i: (l, 0, 0)),
                  pl.BlockSpec((None, KV_LORA, ATT_HEADS * HEAD_PAD), lambda i: (l, 0, 0)),
                  pl.BlockSpec((3, LANE), const),
                  pl.BlockSpec((3, LANE), const)],
        out_specs=[pl.BlockSpec((ATT_HEADS, TM, HEAD_PAD), lambda i: (0, i, 0)),
                   pl.BlockSpec((ATT_HEADS, TM, HEAD_PAD), lambda i: (0, i, 0)),
                   pl.BlockSpec((ATT_HEADS, TM, V_DIM), lambda i: (0, i, 0))],
        out_shape=[jax.ShapeDtypeStruct((ATT_HEADS, TP, HEAD_PAD), BF16),
                   jax.ShapeDtypeStruct((ATT_HEADS, TP, HEAD_PAD), BF16),
                   jax.ShapeDtypeStruct((ATT_HEADS, TP, V_DIM), BF16)],
        compiler_params=_cparams(("parallel",)),
        name="mla_prep",
    )(p, cosp, sinp, glq, glkv, wq, wkv, gq, gk)


def _attn_kernel(q_ref, k_ref, v_ref, o_ref):
    q = q_ref[0]
    m = jnp.full((TQ, 1), -jnp.inf, F32)
    l = jnp.zeros((TQ, 1), F32)
    acc = jnp.zeros((TQ, V_DIM), F32)
    for k0 in range(0, TP, TK):
        k1 = min(k0 + TK, TP)
        s = lax.dot_general(q, k_ref[0, k0:k1, :], (((1,), (1,)), ((), ())), preferred_element_type=F32)
        if k1 > T:
            col = k0 + lax.broadcasted_iota(jnp.int32, (1, k1 - k0), 1)
            s = jnp.where(col < T, s, -jnp.inf)
        m_new = jnp.maximum(m, jnp.max(s, axis=-1, keepdims=True))
        alpha = jnp.exp2(m - m_new)
        pr = jnp.exp2(s - m_new)
        l = alpha * l + jnp.sum(pr, axis=-1, keepdims=True)
        acc = alpha * acc + _dot(pr.astype(BF16), v_ref[0, k0:k1, :])
        m = m_new
    o_ref[...] = acc / l


def attention(q, k, v):
    return pl.pallas_call(
        _attn_kernel,
        grid=(ATT_HEADS, TP // TQ),
        in_specs=[pl.BlockSpec((1, TQ, HEAD_PAD), lambda h, i: (h, i, 0)),
                  pl.BlockSpec((1, TP, HEAD_PAD), lambda h, i: (h, 0, 0)),
                  pl.BlockSpec((1, TP, V_DIM), lambda h, i: (h, 0, 0))],
        out_specs=pl.BlockSpec((TQ, V_DIM), lambda h, i: (i, h)),
        out_shape=jax.ShapeDtypeStruct((TP, ATT_WIDTH), F32),
        compiler_params=_cparams(("parallel", "parallel")),
        name="attention",
    )(q, k, v)


def _short_conv(x, prev_row, next_row, w, b, t0):
    rows = x.shape[0]
    r = lax.broadcasted_iota(jnp.int32, (rows, 1), 0)
    t = t0 + r
    xm = jnp.where(r == 0, prev_row, pltpu.roll(x, 1, 0))
    xm = jnp.where(t == 0, 0.0, xm)
    xp = jnp.where(r == rows - 1, next_row, pltpu.roll(x, rows - 1, 0))
    xp = jnp.where(t >= T - 1, 0.0, xp)
    return xm * w[0:1] + x * w[1:2] + xp * w[2:3] + b


def _hy_pre_kernel(x0_ref, x1_ref, v_ref, x0p_ref, x1p_ref, vp_ref, x0n_ref, x1n_ref, vn_ref,
                   w0_ref, w1_ref, wv_ref, b0_ref, b1_ref, bv_ref, z_ref, u_ref):
    t0 = pl.program_id(0) * TH
    valid = (t0 + lax.broadcasted_iota(jnp.int32, (TH, 1), 0)) < T
    u0 = _short_conv(x0_ref[...], x0p_ref[7:8], x0n_ref[0:1], w0_ref[...], b0_ref[...], t0)
    u1 = _short_conv(x1_ref[...], x1p_ref[7:8], x1n_ref[0:1], w1_ref[...], b1_ref[...], t0)
    uv = _short_conv(v_ref[...], vp_ref[7:8], vn_ref[0:1], wv_ref[...], bv_ref[...], t0)
    z = jnp.where(valid, uv * u1, 0.0)
    u = jnp.where(valid, u0, 0.0)
    for c in range(z_ref.shape[0]):
        z_ref[c] = z[:, c * LANE:(c + 1) * LANE]
        u_ref[c] = u[:, c * LANE:(c + 1) * LANE]


def hyena_pre(p, conv_w, conv_b):
    cw = 1024
    ncw = HY_WIDTH // cw
    off = P_ATT // cw
    halo = TH // 8
    last8 = TP // 8 - 1

    def main(s):
        return pl.BlockSpec((TH, cw), lambda i, j: (i, off + s * ncw + j))

    def prev(s):
        return pl.BlockSpec((8, cw), lambda i, j: (jnp.maximum(i * halo - 1, 0), off + s * ncw + j))

    def nxt(s):
        return pl.BlockSpec((8, cw), lambda i, j: (jnp.minimum((i + 1) * halo, last8), off + s * ncw + j))

    def par(s, rows):
        return pl.BlockSpec((rows, cw), lambda i, j: (0, s * ncw + j))

    out_spec = pl.BlockSpec((cw // LANE, TH, LANE), lambda i, j: (j, i, 0))
    out_sds = jax.ShapeDtypeStruct((NCB, TP, LANE), F32)
    return pl.pallas_call(
        _hy_pre_kernel,
        grid=(pl.cdiv(TP, TH), ncw),
        in_specs=[main(0), main(1), main(2), prev(0), prev(1), prev(2), nxt(0), nxt(1), nxt(2),
                  par(0, 3), par(1, 3), par(2, 3), par(0, 1), par(1, 1), par(2, 1)],
        out_specs=[out_spec, out_spec],
        out_shape=[out_sds, out_sds],
        compiler_params=_cparams(("parallel", "parallel")),
        name="hyena_pre",
    )(p, p, p, p, p, p, p, p, p, conv_w, conv_w, conv_w, conv_b, conv_b, conv_b)


def _filter_kernel(zt_ref, aux_ref, w1_ref, b1_ref, f1_ref, w2_ref, b2_ref, f2_ref, w3_ref, dl_ref,
                   hf_ref, hb_ref):
    hi = lax.Precision.HIGHEST
    h = jnp.sin(f1_ref[...] * (jnp.dot(w1_ref[...], zt_ref[...], precision=hi,
                                       preferred_element_type=F32) + b1_ref[...]))
    h = jnp.sin(f2_ref[...] * (jnp.dot(w2_ref[...], h, precision=hi,
                                       preferred_element_type=F32) + b2_ref[...]))
    hh = lax.dot_general(h.astype(BF16), w3_ref[...], (((0,), (0,)), ((), ())),
                         preferred_element_type=F32)
    aux = aux_ref[...]
    decay = jnp.exp(-aux[:, 0:1] * dl_ref[...])
    hf = hh[:, :HY_WIDTH] * (decay * aux[:, 1:2])
    hb = hh[:, HY_WIDTH:] * (decay * aux[:, 2:3])
    for c in range(NCB):
        hf_ref[c] = hf[:, c * LANE:(c + 1) * LANE].astype(BF16)
        hb_ref[c] = hb[:, c * LANE:(c + 1) * LANE].astype(BF16)


def hyena_filter(zt, aux, w1t, b1, f1, w2t, b2, f2, w3, deltas):
    const = lambda i: (0, 0)
    out_spec = pl.BlockSpec((NCB, TM, LANE), lambda i: (0, i, 0))
    out_sds = jax.ShapeDtypeStruct((NCB, TP, LANE), BF16)
    col = pl.BlockSpec((FILT_ORDER, 1), const)
    return pl.pallas_call(
        _filter_kernel,
        grid=(TP // TM,),
        in_specs=[pl.BlockSpec((FILT_ORDER, TM), lambda i: (0, i)),
                  pl.BlockSpec((TM, 8), lambda i: (i, 0)),
                  pl.BlockSpec((FILT_ORDER, FILT_ORDER), const), col, col,
                  pl.BlockSpec((FILT_ORDER, FILT_ORDER), const), col, col,
                  pl.BlockSpec((FILT_ORDER, 2 * HY_WIDTH), const),
                  pl.BlockSpec((1, HY_WIDTH), const)],
        out_specs=[out_spec, out_spec],
        out_shape=[out_sds, out_sds],
        compiler_params=_cparams(("parallel",)),
        name="hyena_filter",
    )(zt, aux, w1t, b1, f1, w2t, b2, f2, w3, deltas)


def _stage_a(x_ref, fa_ref, zs_ref, as_ref):
    for n1 in range(NB):
        zs_ref[n1 * PZ:n1 * PZ + N2, :] = x_ref[0, n1 * N2:(n1 + 1) * N2, :].astype(F32)
    zs_ref[NB * PZ:, :] = jnp.zeros(((NBP - NB) * PZ, LANE), F32)
    kpad = jnp.zeros((fa_ref.shape[1] - NBP, 2 * LANE), F32)

    def body(j, carry):
        n2 = 2 * j
        x = jnp.concatenate([zs_ref[pl.ds(n2, NBP, stride=PZ), :],
                             zs_ref[pl.ds(n2 + 1, NBP, stride=PZ), :]], axis=1)
        r = _dot(fa_ref[...], jnp.concatenate([x, kpad], axis=0).astype(BF16))
        off = pl.multiple_of(n2 * PA, 8)
        as_ref[pl.ds(off, PA), :] = r[:PA, :LANE]
        as_ref[pl.ds(off + PA, PA), :] = r[:PA, LANE:]
        return carry

    lax.fori_loop(0, N2 // 2, body, 0, unroll=4)


def _stage_b_fwd(as_ref, fb, k1):
    ar = as_ref[pl.ds(k1, N2, stride=PA), :]
    ai = as_ref[pl.ds(IM0 + k1, N2, stride=PA), :]
    return _dot(fb, jnp.concatenate([ar, ai], axis=0).astype(BF16))


def _filter_spectrum_kernel(hf_ref, hb_ref, fa_ref, fb_ref, o_ref, zs_ref, af_ref, ab_ref):
    kc = pl.program_id(1)

    @pl.when(kc == 0)
    def _():
        _stage_a(hf_ref, fa_ref, zs_ref, af_ref)
        _stage_a(hb_ref, fa_ref, zs_ref, ab_ref)

    for k in range(KC):
        k1 = kc * KC + k
        sf = _stage_b_fwd(af_ref, fb_ref[k1], k1)
        sb = _stage_b_fwd(ab_ref, fb_ref[k1], k1)
        o_ref[0, k, :N2, :] = (sf[:N2] + sb[:N2]).astype(BF16)
        o_ref[0, k, N2:, :] = (sf[N2:] - sb[N2:]).astype(BF16)


def filter_spectrum(hf, hb, fa, fb):
    blk = pl.BlockSpec((1, TP, LANE), lambda c, k: (c, 0, 0))
    return pl.pallas_call(
        _filter_spectrum_kernel,
        grid=(NCB, NKC),
        in_specs=[blk, blk,
                  pl.BlockSpec(fa.shape, lambda c, k: (0, 0)),
                  pl.BlockSpec(fb.shape, lambda c, k: (0, 0, 0), pipeline_mode=pl.Buffered(1))],
        out_specs=pl.BlockSpec((1, KC, 2 * N2, LANE), lambda c, k: (c, k, 0, 0)),
        out_shape=jax.ShapeDtypeStruct((NCB, KH, 2 * N2, LANE), BF16),
        scratch_shapes=[pltpu.VMEM((NBP * PZ, LANE), F32),
                        pltpu.VMEM((N2 * PA, LANE), F32),
                        pltpu.VMEM((N2 * PA, LANE), F32)],
        compiler_params=_cparams(("parallel", "arbitrary")),
        name="filter_spectrum",
    )(hf, hb, fa, fb)


def _long_conv_kernel(z_ref, kf_ref, fa_ref, fb_ref, ga_ref, y_ref, zs_ref, as_ref, bs_ref):
    kc = pl.program_id(1)

    @pl.when(kc == 0)
    def _():
        _stage_a(z_ref, fa_ref, zs_ref, as_ref)
        bs_ref[KH * PB:, :] = jnp.zeros(((NBP - KH) * PB, LANE), F32)

    for k in range(KC):
        k1 = kc * KC + k
        xs = _stage_b_fwd(as_ref, fb_ref[k1], k1)
        xr, xi = xs[:N2], xs[N2:]
        fr, fi = kf_ref[0, k, :N2, :].astype(F32), kf_ref[0, k, N2:, :].astype(F32)
        ys = jnp.concatenate([xr * fr - xi * fi, xr * fi + xi * fr], axis=0).astype(BF16)
        bs_ref[pl.ds(pl.multiple_of(k1 * PB, 8), 2 * N2), :] = lax.dot_general(
            fb_ref[k1], ys, (((0,), (0,)), ((), ())), preferred_element_type=F32)

    @pl.when(kc == NKC - 1)
    def _():
        kpad = jnp.zeros((LANE - NBP, 2 * LANE), F32)

        def body(j, carry):
            n2 = 2 * j
            cr = jnp.concatenate([bs_ref[pl.ds(n2, NBP, stride=PB), :],
                                  bs_ref[pl.ds(n2 + 1, NBP, stride=PB), :]], axis=1)
            ci = jnp.concatenate([bs_ref[pl.ds(N2 + n2, NBP, stride=PB), :],
                                  bs_ref[pl.ds(N2 + n2 + 1, NBP, stride=PB), :]], axis=1)
            c = jnp.concatenate([cr, kpad, ci, kpad], axis=0).astype(BF16)
            y = _dot(ga_ref[...], c)
            zs_ref[pl.ds(n2, NBP, stride=PZ), :] = y[:, :LANE]
            zs_ref[pl.ds(n2 + 1, NBP, stride=PZ), :] = y[:, LANE:]
            return carry

        lax.fori_loop(0, N2 // 2, body, 0, unroll=4)
        for n1 in range(NB):
            y_ref[0, n1 * N2:(n1 + 1) * N2, :] = zs_ref[n1 * PZ:n1 * PZ + N2, :]


def long_conv(z, kf, fa, fb, ga):
    blk = pl.BlockSpec((1, TP, LANE), lambda c, k: (c, 0, 0))
    return pl.pallas_call(
        _long_conv_kernel,
        grid=(NCB, NKC),
        in_specs=[blk,
                  pl.BlockSpec((1, KC, 2 * N2, LANE), lambda c, k: (c, k, 0, 0)),
                  pl.BlockSpec(fa.shape, lambda c, k: (0, 0)),
                  pl.BlockSpec(fb.shape, lambda c, k: (0, 0, 0), pipeline_mode=pl.Buffered(1)),
                  pl.BlockSpec(ga.shape, lambda c, k: (0, 0))],
        out_specs=blk,
        out_shape=jax.ShapeDtypeStruct((NCB, TP, LANE), F32),
        scratch_shapes=[pltpu.VMEM((NBP * PZ, LANE), F32),
                        pltpu.VMEM((N2 * PA, LANE), F32),
                        pltpu.VMEM((NBP * PB, LANE), F32)],
        compiler_params=_cparams(("parallel", "arbitrary")),
        name="long_conv",
    )(z, kf, fa, fb, ga)


@functools.lru_cache(maxsize=None)
def _dft_constants():
    k1 = np.arange(KH, dtype=np.int64)

    n1 = np.arange(NB, dtype=np.int64)
    ang = 2.0 * np.pi * ((k1[:, None] * n1[None, :]) % N1) / N1
    fa = np.zeros((PA + 8, LANE))
    fa[:KH, :NB] = np.cos(ang)
    fa[IM0:IM0 + KH, :NB] = -np.sin(ang)

    n2 = np.arange(N2, dtype=np.int64)
    k2 = np.arange(N2, dtype=np.int64)
    freq = k1[:, None, None] + N1 * k2[None, :, None]
    theta = 2.0 * np.pi * ((freq * n2[None, None, :]) % NFFT) / NFFT
    c, s = np.cos(theta), np.sin(theta)
    fb = np.concatenate([np.concatenate([c, s], axis=2), np.concatenate([-s, c], axis=2)], axis=1)

    herm = np.where((k1 == 0) | (k1 == N1 // 2), 1.0, 2.0) / NFFT
    phi = 2.0 * np.pi * ((n1[:, None] * k1[None, :]) % N1) / N1
    ga = np.zeros((NBP, 2 * LANE))
    ga[:NB, :KH] = np.cos(phi) * herm[None, :]
    ga[:NB, LANE:LANE + KH] = -np.sin(phi) * herm[None, :]
    return tuple(m.astype(np.float32) for m in (fa, fb, ga))


@functools.lru_cache(maxsize=None)
def _position_tables():
    pos = np.arange(TP, dtype=np.float64)
    inv = ROPE_THETA ** (-np.arange(0, QK_ROPE, 2, dtype=np.float64) / QK_ROPE)
    ang = pos[:, None] * inv[None, :]
    ang = np.concatenate([ang, ang], axis=-1)
    pad = np.zeros((TP, LANE - QK_ROPE))
    cosp = np.concatenate([np.cos(ang), pad], axis=-1)
    sinp = np.concatenate([np.sin(ang), pad], axis=-1)

    p = np.minimum(np.arange(TP), T - 1).astype(np.float64)
    tl = p / (T - 1)
    freqs = np.linspace(1e-4, FILT_BANDS - 1, FILT_BANDS)
    fang = (2.0 * np.pi * p / T)[None, :] * freqs[:, None]
    zt = np.zeros((FILT_ORDER, TP))
    zt[0] = tl
    zt[1:1 + FILT_BANDS] = np.cos(fang)
    zt[1 + FILT_BANDS:FILT_EMB] = -np.sin(fang)
    live = np.arange(TP) < T
    aux = np.zeros((TP, 8))
    aux[:, 0] = tl
    aux[:, 1] = live
    aux[:, 2] = live & (np.arange(TP) >= 1)
    max_decay = math.log(DECAY_TARGET) / FAST_DECAY_PCT
    min_decay = math.log(DECAY_TARGET) / SLOW_DECAY_PCT
    deltas = np.abs(np.linspace(min_decay, max_decay, HY_WIDTH))[None, :]
    return tuple(m.astype(np.float32) for m in (cosp, sinp, zt, aux, deltas))


def _rot_cols(w):
    half = QK_ROPE // 2
    return jnp.concatenate([-w[..., half:], w[..., :half]], axis=-1)


def _rope_gain_rows(g):
    half = QK_ROPE // 2
    gr = g[QK_NOPE:]
    pad = jnp.zeros((LANE - QK_ROPE,), F32)
    return jnp.stack([g[:QK_NOPE], jnp.concatenate([gr, pad]),
                      jnp.concatenate([gr[half:], gr[:half], pad])])


def kernel(x, meta_tokens, norm_mix_g, w_in, q_lat_g, kv_lat_g, w_uq, w_ukv, q_norm_g, k_norm_g, conv_w, conv_b, filt_w1, filt_b1, filt_freq1, filt_w2, filt_b2, filt_freq2, filt_w3, hy_skip, attn_out_g, hy_out_g, w_out, norm_ffn_g, w_gate, w_up, w_down):
    assert x.shape == (1, SEQ, D_MODEL)
    h = jnp.concatenate([meta_tokens.astype(F32), x[0], jnp.zeros((TP - T, D_MODEL), F32)], axis=0)

    cosp, sinp, zt, aux, deltas = (jnp.asarray(m) for m in _position_tables())
    fa, fb, ga = (jnp.asarray(m).astype(BF16) for m in _dft_constants())
    epad = FILT_ORDER - FILT_EMB
    col = lambda v: v[:, None]

    o2, o3 = Q_LORA + KV_LORA, Q_LORA + KV_LORA + QK_ROPE
    lane_pad = lambda a, lo, hi: jnp.pad(a, ((0, 0), (0, 0), (lo, hi)))
    w_in_big = (lane_pad(w_in[..., :o3], 0, P_COLS - o3)
                + lane_pad(_rot_cols(w_in[..., o2:o3]), o3, P_COLS - o3 - QK_ROPE)
                + lane_pad(w_in[..., o3:], P_ATT, 0)).astype(BF16)
    wq = w_uq.astype(BF16).reshape(DEPTH, Q_LORA, ATT_HEADS, QK_DIM)
    wq = jnp.concatenate([wq, _rot_cols(wq[..., QK_NOPE:])], axis=-1)
    wq = wq.reshape(DEPTH, Q_LORA, ATT_HEADS * HEAD_PAD)
    wkv = w_ukv.astype(BF16)
    w_out_b, w_gate_b, w_up_b, w_down_b = (w.astype(BF16) for w in (w_out, w_gate, w_up, w_down))

    for l in range(DEPTH):
        p = norm_matmul(h, norm_mix_g[l][None], w_in_big, l, 2048)
        q, k, v = mla_prep(p, cosp, sinp, q_lat_g[l][None], kv_lat_g[l][None], wq, wkv, l,
                           _rope_gain_rows(q_norm_g[l]), _rope_gain_rows(k_norm_g[l]))
        att = attention(q, k, v)

        z, u = hyena_pre(p, conv_w[l], conv_b[l][None])
        hf, hb = hyena_filter(
            zt, aux,
            jnp.pad(filt_w1[l].T, ((0, 0), (0, epad))), col(filt_b1[l]), col(filt_freq1[l]),
            filt_w2[l].T, col(filt_b2[l]), col(filt_freq2[l]),
            filt_w3[l].astype(BF16), deltas)
        kf = filter_spectrum(hf, hb, fa, fb)
        y = long_conv(z, kf, fa, fb, ga)

        h = mix_matmul_residual(att, y, z, u, hy_skip[l][None], attn_out_g[l][None], hy_out_g[l][None],
                                w_out_b, l, h)

        act = norm_swiglu(h, norm_ffn_g[l][None], w_gate_b, w_up_b, l, 512)
        h = matmul_residual(act, w_down_b, l, h)

    return h[N_META:T][None]
```

```python
import functools
import math

import numpy as np
import jax
import jax.numpy as jnp
from jax import lax
from jax.experimental import pallas as pl
from jax.experimental.pallas import tpu as pltpu

F32 = jnp.float32
BF16 = jnp.bfloat16

D_MODEL = 2048
SEQ = 8192
DEPTH = 4
N_META = 16
T = N_META + SEQ
V_DIM = 128
QK_NOPE = 128
QK_ROPE = 64
QK_DIM = QK_NOPE + QK_ROPE
ATT_HEADS = 8
ATT_WIDTH = ATT_HEADS * V_DIM
Q_LORA = 512
KV_LORA = 256
HY_WIDTH = 1024
FILT_EMB = 33
FILT_BANDS = 16
FILT_ORDER = 64
DECAY_TARGET = 1e-2
FAST_DECAY_PCT = 0.3
SLOW_DECAY_PCT = 1.5
D_FF = 5632
ROPE_THETA = 10000.0
EPS = 1e-6

LANE = 128
HEAD_PAD = 2 * LANE
VMEM_LIMIT = 56 * 1024 * 1024

TP = 8320
TM = 640
TM_FFN = 1664
TM_MIX = 320
FFN_TN = 512
TQ = 1040
TK = 2048
P_COLS = 4096
P_ATT = 1024
TH = 1024

N2 = 128
N1 = 130
NFFT = N1 * N2
KH = N1 // 2 + 1
KC = 33
NKC = KH // KC
NB = TP // N2
NBP = 72
NCB = HY_WIDTH // LANE
PZ = 136
PA = 136
IM0 = 68
PB = 264


def _cparams(sem, flags=None):
    return pltpu.CompilerParams(dimension_semantics=sem, vmem_limit_bytes=VMEM_LIMIT, flags=flags)


def _rms(x):
    return lax.rsqrt(jnp.mean(x * x, axis=-1, keepdims=True) + EPS)


def _dot(a, b):
    return jnp.dot(a, b, preferred_element_type=F32)


def _norm_mm_kernel(x_ref, g_ref, w_ref, o_ref):
    x = x_ref[...]
    o_ref[...] = _dot((x * _rms(x) * g_ref[...]).astype(BF16), w_ref[...])


def _layer_spec(l, k, tn):
    return pl.BlockSpec((None, k, tn), lambda i, j: (l, 0, j))


def norm_matmul(x, g, w, l, tn):
    m, k = x.shape
    n = w.shape[2]
    return pl.pallas_call(
        _norm_mm_kernel,
        grid=(n // tn, m // TM),
        in_specs=[pl.BlockSpec((TM, k), lambda j, i: (i, 0)),
                  pl.BlockSpec((1, k), lambda j, i: (0, 0)),
                  pl.BlockSpec((None, k, tn), lambda j, i: (l, 0, j))],
        out_specs=pl.BlockSpec((TM, tn), lambda j, i: (i, j)),
        out_shape=jax.ShapeDtypeStruct((m, n), F32),
        compiler_params=_cparams(("parallel", "parallel")),
        name="norm_matmul",
    )(x, g, w)


def _norm_swiglu_kernel(x_ref, g_ref, w_ref, o_ref, xn_ref):
    @pl.when(pl.program_id(1) == 0)
    def _():
        x = x_ref[...]
        xn_ref[...] = (x * _rms(x) * g_ref[...]).astype(BF16)

    tn = o_ref.shape[1]
    r = _dot(xn_ref[...], w_ref[...])
    a, b = r[:, :tn], r[:, tn:]
    o_ref[...] = (a * jax.nn.sigmoid(a) * b).astype(BF16)


def norm_swiglu(x, g, wgu, l, tn):
    m, k = x.shape
    n = wgu.shape[2] // 2
    return pl.pallas_call(
        _norm_swiglu_kernel,
        grid=(m // TM_FFN, n // tn),
        in_specs=[pl.BlockSpec((TM_FFN, k), lambda i, j: (i, 0)),
                  pl.BlockSpec((1, k), lambda i, j: (0, 0)),
                  _layer_spec(l, k, 2 * tn)],
        out_specs=pl.BlockSpec((TM_FFN, tn), lambda i, j: (i, j)),
        out_shape=jax.ShapeDtypeStruct((m, n), BF16),
        scratch_shapes=[pltpu.VMEM((TM_FFN, k), BF16)],
        compiler_params=_cparams(("parallel", "arbitrary")),
        name="norm_swiglu",
    )(x, g, wgu)


def _mm_res_kernel(a_ref, w_ref, r_ref, o_ref):
    o_ref[...] = r_ref[...] + _dot(a_ref[...], w_ref[...])


def matmul_residual(a, w, l, res):
    m, k = a.shape
    n = w.shape[2]
    return pl.pallas_call(
        _mm_res_kernel,
        grid=(m // TM_MIX,),
        in_specs=[pl.BlockSpec((TM_MIX, k), lambda i: (i, 0)),
                  pl.BlockSpec((None, k, n), lambda i: (l, 0, 0), pipeline_mode=pl.Buffered(1)),
                  pl.BlockSpec((TM_MIX, n), lambda i: (i, 0))],
        out_specs=pl.BlockSpec((TM_MIX, n), lambda i: (i, 0)),
        out_shape=jax.ShapeDtypeStruct((m, n), F32),
        compiler_params=_cparams(("parallel",)),
        name="matmul_residual",
    )(a, w, res)


def _mix_mm_res_kernel(a_ref, y_ref, z_ref, u_ref, d_ref, ga_ref, gy_ref, w_ref, r_ref, o_ref):
    a = a_ref[...]
    cat = lambda ref: jnp.concatenate([ref[c] for c in range(NCB)], axis=-1)
    y = (cat(y_ref) + cat(z_ref) * d_ref[...]) * cat(u_ref)
    mix = jnp.concatenate([(a * _rms(a) * ga_ref[...]).astype(BF16),
                           (y * _rms(y) * gy_ref[...]).astype(BF16)], axis=-1)
    o_ref[...] = r_ref[...] + _dot(mix, w_ref[...])


def mix_matmul_residual(a, y, z, u, d, ga, gy, w, l, res):
    m = a.shape[0]
    k, n = w.shape[1:]
    blk = pl.BlockSpec((NCB, TM_MIX, LANE), lambda i: (0, i, 0))
    row = lambda width: pl.BlockSpec((1, width), lambda i: (0, 0))
    return pl.pallas_call(
        _mix_mm_res_kernel,
        grid=(m // TM_MIX,),
        in_specs=[pl.BlockSpec((TM_MIX, ATT_WIDTH), lambda i: (i, 0)), blk, blk, blk,
                  row(HY_WIDTH), row(ATT_WIDTH), row(HY_WIDTH),
                  pl.BlockSpec((None, k, n), lambda i: (l, 0, 0), pipeline_mode=pl.Buffered(1)),
                  pl.BlockSpec((TM_MIX, n), lambda i: (i, 0))],
        out_specs=pl.BlockSpec((TM_MIX, n), lambda i: (i, 0)),
        out_shape=jax.ShapeDtypeStruct((m, n), F32),
        compiler_params=_cparams(("parallel",)),
        name="mix_matmul_residual",
    )(a, y, z, u, d, ga, gy, w, res)


def _mla_prep_kernel(p_ref, cos_ref, sin_ref, glq_ref, glkv_ref, wq_ref, wkv_ref, gq_ref, gk_ref,
                     q_ref, k_ref, v_ref):
    p = p_ref[...]
    cq = p[:, :Q_LORA]
    ckv = p[:, Q_LORA:Q_LORA + KV_LORA]
    kr = p[:, Q_LORA + KV_LORA:Q_LORA + KV_LORA + LANE]
    qp = _dot((cq * _rms(cq) * glq_ref[...]).astype(BF16), wq_ref[...])
    kvp = _dot((ckv * _rms(ckv) * glkv_ref[...]).astype(BF16), wkv_ref[...])
    cosp = cos_ref[...]
    sinp = sin_ref[...]
    rope_lane = lax.broadcasted_iota(jnp.int32, (1, LANE), 1) < QK_ROPE
    gq = gq_ref[...]
    gk = gk_ref[...]
    q_cos, q_sin = cosp * gq[1:2], sinp * gq[2:3]
    k_cos, k_sin = cosp * gk[1:2], sinp * gk[2:3]
    kr_ss = jnp.sum(jnp.where(rope_lane, kr * kr, 0.0), axis=-1, keepdims=True)
    kr_rot = kr * k_cos + pltpu.roll(kr, QK_ROPE, 1) * k_sin
    q_scale = QK_DIM ** -0.5 * math.log2(math.e)
    for h in range(ATT_HEADS):
        qn = qp[:, h * HEAD_PAD:h * HEAD_PAD + LANE]
        qa = qp[:, h * HEAD_PAD + LANE:(h + 1) * HEAD_PAD]
        ss = (jnp.sum(qn * qn, axis=-1, keepdims=True)
              + jnp.sum(jnp.where(rope_lane, qa * qa, 0.0), axis=-1, keepdims=True))
        rq = lax.rsqrt(ss * (1.0 / QK_DIM) + EPS) * q_scale
        q_ref[h, :, :LANE] = (qn * rq * gq[0:1]).astype(BF16)
        q_ref[h, :, LANE:] = ((qa * q_cos + pltpu.roll(qa, QK_ROPE, 1) * q_sin) * rq).astype(BF16)
        kn = kvp[:, h * HEAD_PAD:h * HEAD_PAD + LANE]
        rk = lax.rsqrt((jnp.sum(kn * kn, axis=-1, keepdims=True) + kr_ss) * (1.0 / QK_DIM) + EPS)
        k_ref[h, :, :LANE] = (kn * rk * gk[0:1]).astype(BF16)
        k_ref[h, :, LANE:] = (kr_rot * rk).astype(BF16)
        v_ref[h] = kvp[:, h * HEAD_PAD + LANE:(h + 1) * HEAD_PAD].astype(BF16)


def mla_prep(p, cosp, sinp, glq, glkv, wq, wkv, l, gq, gk):
    const = lambda i: (0, 0)
    return pl.pallas_call(
        _mla_prep_kernel,
        grid=(TP // TM,),
        in_specs=[pl.BlockSpec((TM, P_ATT), lambda i: (i, 0)),
                  pl.BlockSpec((TM, LANE), lambda i: (i, 0)),
                  pl.BlockSpec((TM, LANE), lambda i: (i, 0)),
                  pl.BlockSpec((1, Q_LORA), const),
                  pl.BlockSpec((1, KV_LORA), const),
                  pl.BlockSpec((None, Q_LORA, ATT_HEADS * HEAD_PAD), lambda i: (l, 0, 0)),
                  pl.BlockSpec((None, KV_LORA, ATT_HEADS * HEAD_PAD), lambda i: (l, 0, 0)),
                  pl.BlockSpec((3, LANE), const),
                  pl.BlockSpec((3, LANE), const)],
        out_specs=[pl.BlockSpec((ATT_HEADS, TM, HEAD_PAD), lambda i: (0, i, 0)),
                   pl.BlockSpec((ATT_HEADS, TM, HEAD_PAD), lambda i: (0, i, 0)),
                   pl.BlockSpec((ATT_HEADS, TM, V_DIM), lambda i: (0, i, 0))],
        out_shape=[jax.ShapeDtypeStruct((ATT_HEADS, TP, HEAD_PAD), BF16),
                   jax.ShapeDtypeStruct((ATT_HEADS, TP, HEAD_PAD), BF16),
                   jax.ShapeDtypeStruct((ATT_HEADS, TP, V_DIM), BF16)],
        compiler_params=_cparams(("parallel",)),
        name="mla_prep",
    )(p, cosp, sinp, glq, glkv, wq, wkv, gq, gk)


def _attn_kernel(q_ref, k_ref, v_ref, o_ref):
    q = q_ref[0]
    m = jnp.full((TQ, 1), -jnp.inf, F32)
    l = jnp.zeros((TQ, 1), F32)
    acc = jnp.zeros((TQ, V_DIM), F32)
    for k0 in range(0, TP, TK):
        k1 = min(k0 + TK, TP)
        s = lax.dot_general(q, k_ref[0, k0:k1, :], (((1,), (1,)), ((), ())), preferred_element_type=F32)
        if k1 > T:
            col = k0 + lax.broadcasted_iota(jnp.int32, (1, k1 - k0), 1)
            s = jnp.where(col < T, s, -jnp.inf)
        m_new = jnp.maximum(m, jnp.max(s, axis=-1, keepdims=True))
        alpha = jnp.exp2(m - m_new)
        pr = jnp.exp2(s - m_new)
        l = alpha * l + jnp.sum(pr, axis=-1, keepdims=True)
        acc = alpha * acc + _dot(pr.astype(BF16), v_ref[0, k0:k1, :])
        m = m_new
    o_ref[...] = acc / l


def attention(q, k, v):
    return pl.pallas_call(
        _attn_kernel,
        grid=(ATT_HEADS, TP // TQ),
        in_specs=[pl.BlockSpec((1, TQ, HEAD_PAD), lambda h, i: (h, i, 0)),
                  pl.BlockSpec((1, TP, HEAD_PAD), lambda h, i: (h, 0, 0)),
                  pl.BlockSpec((1, TP, V_DIM), lambda h, i: (h, 0, 0))],
        out_specs=pl.BlockSpec((TQ, V_DIM), lambda h, i: (i, h)),
        out_shape=jax.ShapeDtypeStruct((TP, ATT_WIDTH), F32),
        compiler_params=_cparams(("parallel", "parallel")),
        name="attention",
    )(q, k, v)


def _short_conv(x, prev_row, next_row, w, b, t0):
    rows = x.shape[0]
    r = lax.broadcasted_iota(jnp.int32, (rows, 1), 0)
    t = t0 + r
    xm = jnp.where(r == 0, prev_row, pltpu.roll(x, 1, 0))
    xm = jnp.where(t == 0, 0.0, xm)
    xp = jnp.where(r == rows - 1, next_row, pltpu.roll(x, rows - 1, 0))
    xp = jnp.where(t >= T - 1, 0.0, xp)
    return xm * w[0:1] + x * w[1:2] + xp * w[2:3] + b


def _hy_pre_kernel(x0_ref, x1_ref, v_ref, x0p_ref, x1p_ref, vp_ref, x0n_ref, x1n_ref, vn_ref,
                   w0_ref, w1_ref, wv_ref, b0_ref, b1_ref, bv_ref, z_ref, u_ref):
    t0 = pl.program_id(0) * TH
    valid = (t0 + lax.broadcasted_iota(jnp.int32, (TH, 1), 0)) < T
    u0 = _short_conv(x0_ref[...], x0p_ref[7:8], x0n_ref[0:1], w0_ref[...], b0_ref[...], t0)
    u1 = _short_conv(x1_ref[...], x1p_ref[7:8], x1n_ref[0:1], w1_ref[...], b1_ref[...], t0)
    uv = _short_conv(v_ref[...], vp_ref[7:8], vn_ref[0:1], wv_ref[...], bv_ref[...], t0)
    z = jnp.where(valid, uv * u1, 0.0)
    u = jnp.where(valid, u0, 0.0)
    for c in range(z_ref.shape[0]):
        z_ref[c] = z[:, c * LANE:(c + 1) * LANE]
        u_ref[c] = u[:, c * LANE:(c + 1) * LANE]


def hyena_pre(p, conv_w, conv_b):
    cw = 1024
    ncw = HY_WIDTH // cw
    off = P_ATT // cw
    halo = TH // 8
    last8 = TP // 8 - 1

    def main(s):
        return pl.BlockSpec((TH, cw), lambda i, j: (i, off + s * ncw + j))

    def prev(s):
        return pl.BlockSpec((8, cw), lambda i, j: (jnp.maximum(i * halo - 1, 0), off + s * ncw + j))

    def nxt(s):
        return pl.BlockSpec((8, cw), lambda i, j: (jnp.minimum((i + 1) * halo, last8), off + s * ncw + j))

    def par(s, rows):
        return pl.BlockSpec((rows, cw), lambda i, j: (0, s * ncw + j))

    out_spec = pl.BlockSpec((cw // LANE, TH, LANE), lambda i, j: (j, i, 0))
    out_sds = jax.ShapeDtypeStruct((NCB, TP, LANE), F32)
    return pl.pallas_call(
        _hy_pre_kernel,
        grid=(pl.cdiv(TP, TH), ncw),
        in_specs=[main(0), main(1), main(2), prev(0), prev(1), prev(2), nxt(0), nxt(1), nxt(2),
                  par(0, 3), par(1, 3), par(2, 3), par(0, 1), par(1, 1), par(2, 1)],
        out_specs=[out_spec, out_spec],
        out_shape=[out_sds, out_sds],
        compiler_params=_cparams(("parallel", "parallel")),
        name="hyena_pre",
    )(p, p, p, p, p, p, p, p, p, conv_w, conv_w, conv_w, conv_b, conv_b, conv_b)


def _filter_kernel(zt_ref, aux_ref, w1_ref, b1_ref, f1_ref, w2_ref, b2_ref, f2_ref, w3_ref, dl_ref,
                   hf_ref, hb_ref):
    hi = lax.Precision.HIGHEST
    h = jnp.sin(f1_ref[...] * (jnp.dot(w1_ref[...], zt_ref[...], precision=hi,
                                       preferred_element_type=F32) + b1_ref[...]))
    h = jnp.sin(f2_ref[...] * (jnp.dot(w2_ref[...], h, precision=hi,
                                       preferred_element_type=F32) + b2_ref[...]))
    hh = lax.dot_general(h.astype(BF16), w3_ref[...], (((0,), (0,)), ((), ())),
                         preferred_element_type=F32)
    aux = aux_ref[...]
    decay = jnp.exp(-aux[:, 0:1] * dl_ref[...])
    hf = hh[:, :HY_WIDTH] * (decay * aux[:, 1:2])
    hb = hh[:, HY_WIDTH:] * (decay * aux[:, 2:3])
    for c in range(NCB):
        hf_ref[c] = hf[:, c * LANE:(c + 1) * LANE].astype(BF16)
        hb_ref[c] = hb[:, c * LANE:(c + 1) * LANE].astype(BF16)


def hyena_filter(zt, aux, w1t, b1, f1, w2t, b2, f2, w3, deltas):
    const = lambda i: (0, 0)
    out_spec = pl.BlockSpec((NCB, TM, LANE), lambda i: (0, i, 0))
    out_sds = jax.ShapeDtypeStruct((NCB, TP, LANE), BF16)
    col = pl.BlockSpec((FILT_ORDER, 1), const)
    return pl.pallas_call(
        _filter_kernel,
        grid=(TP // TM,),
        in_specs=[pl.BlockSpec((FILT_ORDER, TM), lambda i: (0, i)),
                  pl.BlockSpec((TM, 8), lambda i: (i, 0)),
                  pl.BlockSpec((FILT_ORDER, FILT_ORDER), const), col, col,
                  pl.BlockSpec((FILT_ORDER, FILT_ORDER), const), col, col,
                  pl.BlockSpec((FILT_ORDER, 2 * HY_WIDTH), const),
                  pl.BlockSpec((1, HY_WIDTH), const)],
        out_specs=[out_spec, out_spec],
        out_shape=[out_sds, out_sds],
        compiler_params=_cparams(("parallel",)),
        name="hyena_filter",
    )(zt, aux, w1t, b1, f1, w2t, b2, f2, w3, deltas)


def _stage_a(x_ref, fa_ref, zs_ref, as_ref):
    for n1 in range(NB):
        zs_ref[n1 * PZ:n1 * PZ + N2, :] = x_ref[0, n1 * N2:(n1 + 1) * N2, :].astype(F32)
    zs_ref[NB * PZ:, :] = jnp.zeros(((NBP - NB) * PZ, LANE), F32)
    kpad = jnp.zeros((fa_ref.shape[1] - NBP, 2 * LANE), F32)

    def body(j, carry):
        n2 = 2 * j
        x = jnp.concatenate([zs_ref[pl.ds(n2, NBP, stride=PZ), :],
                             zs_ref[pl.ds(n2 + 1, NBP, stride=PZ), :]], axis=1)
        r = _dot(fa_ref[...], jnp.concatenate([x, kpad], axis=0).astype(BF16))
        off = pl.multiple_of(n2 * PA, 8)
        as_ref[pl.ds(off, PA), :] = r[:PA, :LANE]
        as_ref[pl.ds(off + PA, PA), :] = r[:PA, LANE:]
        return carry

    lax.fori_loop(0, N2 // 2, body, 0, unroll=4)


def _stage_b_fwd(as_ref, fb, k1):
    ar = as_ref[pl.ds(k1, N2, stride=PA), :]
    ai = as_ref[pl.ds(IM0 + k1, N2, stride=PA), :]
    return _dot(fb, jnp.concatenate([ar, ai], axis=0).astype(BF16))


def _filter_spectrum_kernel(hf_ref, hb_ref, fa_ref, fb_ref, o_ref, zs_ref, af_ref, ab_ref):
    kc = pl.program_id(1)

    @pl.when(kc == 0)
    def _():
        _stage_a(hf_ref, fa_ref, zs_ref, af_ref)
        _stage_a(hb_ref, fa_ref, zs_ref, ab_ref)

    for k in range(KC):
        k1 = kc * KC + k
        sf = _stage_b_fwd(af_ref, fb_ref[k1], k1)
        sb = _stage_b_fwd(ab_ref, fb_ref[k1], k1)
        o_ref[0, k, :N2, :] = (sf[:N2] + sb[:N2]).astype(BF16)
        o_ref[0, k, N2:, :] = (sf[N2:] - sb[N2:]).astype(BF16)


def filter_spectrum(hf, hb, fa, fb):
    blk = pl.BlockSpec((1, TP, LANE), lambda c, k: (c, 0, 0))
    return pl.pallas_call(
        _filter_spectrum_kernel,
        grid=(NCB, NKC),
        in_specs=[blk, blk,
                  pl.BlockSpec(fa.shape, lambda c, k: (0, 0)),
                  pl.BlockSpec(fb.shape, lambda c, k: (0, 0, 0), pipeline_mode=pl.Buffered(1))],
        out_specs=pl.BlockSpec((1, KC, 2 * N2, LANE), lambda c, k: (c, k, 0, 0)),
        out_shape=jax.ShapeDtypeStruct((NCB, KH, 2 * N2, LANE), BF16),
        scratch_shapes=[pltpu.VMEM((NBP * PZ, LANE), F32),
                        pltpu.VMEM((N2 * PA, LANE), F32),
                        pltpu.VMEM((N2 * PA, LANE), F32)],
        compiler_params=_cparams(("parallel", "arbitrary")),
        name="filter_spectrum",
    )(hf, hb, fa, fb)


def _long_conv_kernel(z_ref, kf_ref, fa_ref, fb_ref, ga_ref, y_ref, zs_ref, as_ref, bs_ref):
    kc = pl.program_id(1)

    @pl.when(kc == 0)
    def _():
        _stage_a(z_ref, fa_ref, zs_ref, as_ref)
        bs_ref[KH * PB:, :] = jnp.zeros(((NBP - KH) * PB, LANE), F32)

    for k in range(KC):
        k1 = kc * KC + k
        xs = _stage_b_fwd(as_ref, fb_ref[k1], k1)
        xr, xi = xs[:N2], xs[N2:]
        fr, fi = kf_ref[0, k, :N2, :].astype(F32), kf_ref[0, k, N2:, :].astype(F32)
        ys = jnp.concatenate([xr * fr - xi * fi, xr * fi + xi * fr], axis=0).astype(BF16)
        bs_ref[pl.ds(pl.multiple_of(k1 * PB, 8), 2 * N2), :] = lax.dot_general(
            fb_ref[k1], ys, (((0,), (0,)), ((), ())), preferred_element_type=F32)

    @pl.when(kc == NKC - 1)
    def _():
        kpad = jnp.zeros((LANE - NBP, 2 * LANE), F32)

        def body(j, carry):
            n2 = 2 * j
            cr = jnp.concatenate([bs_ref[pl.ds(n2, NBP, stride=PB), :],
                                  bs_ref[pl.ds(n2 + 1, NBP, stride=PB), :]], axis=1)
            ci = jnp.concatenate([bs_ref[pl.ds(N2 + n2, NBP, stride=PB), :],
                                  bs_ref[pl.ds(N2 + n2 + 1, NBP, stride=PB), :]], axis=1)
            c = jnp.concatenate([cr, kpad, ci, kpad], axis=0).astype(BF16)
            y = _dot(ga_ref[...], c)
            zs_ref[pl.ds(n2, NBP, stride=PZ), :] = y[:, :LANE]
            zs_ref[pl.ds(n2 + 1, NBP, stride=PZ), :] = y[:, LANE:]
            return carry

        lax.fori_loop(0, N2 // 2, body, 0, unroll=4)
        for n1 in range(NB):
            y_ref[0, n1 * N2:(n1 + 1) * N2, :] = zs_ref[n1 * PZ:n1 * PZ + N2, :]


def long_conv(z, kf, fa, fb, ga):
    blk = pl.BlockSpec((1, TP, LANE), lambda c, k: (c, 0, 0))
    return pl.pallas_call(
        _long_conv_kernel,
        grid=(NCB, NKC),
        in_specs=[blk,
                  pl.BlockSpec((1, KC, 2 * N2, LANE), lambda c, k: (c, k, 0, 0)),
                  pl.BlockSpec(fa.shape, lambda c, k: (0, 0)),
                  pl.BlockSpec(fb.shape, lambda c, k: (0, 0, 0), pipeline_mode=pl.Buffered(1)),
                  pl.BlockSpec(ga.shape, lambda c, k: (0, 0))],
        out_specs=blk,
        out_shape=jax.ShapeDtypeStruct((NCB, TP, LANE), F32),
        scratch_shapes=[pltpu.VMEM((NBP * PZ, LANE), F32),
                        pltpu.VMEM((N2 * PA, LANE), F32),
                        pltpu.VMEM((NBP * PB, LANE), F32)],
        compiler_params=_cparams(("parallel", "arbitrary")),
        name="long_conv",
    )(z, kf, fa, fb, ga)


@functools.lru_cache(maxsize=None)
def _dft_constants():
    k1 = np.arange(KH, dtype=np.int64)

    n1 = np.arange(NB, dtype=np.int64)
    ang = 2.0 * np.pi * ((k1[:, None] * n1[None, :]) % N1) / N1
    fa = np.zeros((PA + 8, LANE))
    fa[:KH, :NB] = np.cos(ang)
    fa[IM0:IM0 + KH, :NB] = -np.sin(ang)

    n2 = np.arange(N2, dtype=np.int64)
    k2 = np.arange(N2, dtype=np.int64)
    freq = k1[:, None, None] + N1 * k2[None, :, None]
    theta = 2.0 * np.pi * ((freq * n2[None, None, :]) % NFFT) / NFFT
    c, s = np.cos(theta), np.sin(theta)
    fb = np.concatenate([np.concatenate([c, s], axis=2), np.concatenate([-s, c], axis=2)], axis=1)

    herm = np.where((k1 == 0) | (k1 == N1 // 2), 1.0, 2.0) / NFFT
    phi = 2.0 * np.pi * ((n1[:, None] * k1[None, :]) % N1) / N1
    ga = np.zeros((NBP, 2 * LANE))
    ga[:NB, :KH] = np.cos(phi) * herm[None, :]
    ga[:NB, LANE:LANE + KH] = -np.sin(phi) * herm[None, :]
    return tuple(m.astype(np.float32) for m in (fa, fb, ga))


@functools.lru_cache(maxsize=None)
def _position_tables():
    pos = np.arange(TP, dtype=np.float64)
    inv = ROPE_THETA ** (-np.arange(0, QK_ROPE, 2, dtype=np.float64) / QK_ROPE)
    ang = pos[:, None] * inv[None, :]
    ang = np.concatenate([ang, ang], axis=-1)
    pad = np.zeros((TP, LANE - QK_ROPE))
    cosp = np.concatenate([np.cos(ang), pad], axis=-1)
    sinp = np.concatenate([np.sin(ang), pad], axis=-1)

    p = np.minimum(np.arange(TP), T - 1).astype(np.float64)
    tl = p / (T - 1)
    freqs = np.linspace(1e-4, FILT_BANDS - 1, FILT_BANDS)
    fang = (2.0 * np.pi * p / T)[None, :] * freqs[:, None]
    zt = np.zeros((FILT_ORDER, TP))
    zt[0] = tl
    zt[1:1 + FILT_BANDS] = np.cos(fang)
    zt[1 + FILT_BANDS:FILT_EMB] = -np.sin(fang)
    live = np.arange(TP) < T
    aux = np.zeros((TP, 8))
    aux[:, 0] = tl
    aux[:, 1] = live
    aux[:, 2] = live & (np.arange(TP) >= 1)
    max_decay = math.log(DECAY_TARGET) / FAST_DECAY_PCT
    min_decay = math.log(DECAY_TARGET) / SLOW_DECAY_PCT
    deltas = np.abs(np.linspace(min_decay, max_decay, HY_WIDTH))[None, :]
    return tuple(m.astype(np.float32) for m in (cosp, sinp, zt, aux, deltas))


def _rot_cols(w):
    half = QK_ROPE // 2
    return jnp.concatenate([-w[..., half:], w[..., :half]], axis=-1)


def _rope_gain_rows(g):
    half = QK_ROPE // 2
    gr = g[QK_NOPE:]
    pad = jnp.zeros((LANE - QK_ROPE,), F32)
    return jnp.stack([g[:QK_NOPE], jnp.concatenate([gr, pad]),
                      jnp.concatenate([gr[half:], gr[:half], pad])])


def kernel(x, meta_tokens, norm_mix_g, w_in, q_lat_g, kv_lat_g, w_uq, w_ukv, q_norm_g, k_norm_g, conv_w, conv_b, filt_w1, filt_b1, filt_freq1, filt_w2, filt_b2, filt_freq2, filt_w3, hy_skip, attn_out_g, hy_out_g, w_out, norm_ffn_g, w_gate, w_up, w_down):
    assert x.shape == (1, SEQ, D_MODEL)
    h = jnp.concatenate([meta_tokens.astype(F32), x[0], jnp.zeros((TP - T, D_MODEL), F32)], axis=0)

    cosp, sinp, zt, aux, deltas = (jnp.asarray(m) for m in _position_tables())
    fa, fb, ga = (jnp.asarray(m).astype(BF16) for m in _dft_constants())
    epad = FILT_ORDER - FILT_EMB
    col = lambda v: v[:, None]

    o2, o3 = Q_LORA + KV_LORA, Q_LORA + KV_LORA + QK_ROPE
    lane_pad = lambda a, lo, hi: jnp.pad(a, ((0, 0), (0, 0), (lo, hi)))
    w_in_big = (lane_pad(w_in[..., :o3], 0, P_COLS - o3)
                + lane_pad(_rot_cols(w_in[..., o2:o3]), o3, P_COLS - o3 - QK_ROPE)
                + lane_pad(w_in[..., o3:], P_ATT, 0)).astype(BF16)
    wq = w_uq.astype(BF16).reshape(DEPTH, Q_LORA, ATT_HEADS, QK_DIM)
    wq = jnp.concatenate([wq, _rot_cols(wq[..., QK_NOPE:])], axis=-1)
    wq = wq.reshape(DEPTH, Q_LORA, ATT_HEADS * HEAD_PAD)
    wkv = w_ukv.astype(BF16)
    w_out_b, w_down_b = w_out.astype(BF16), w_down.astype(BF16)
    tiles = lambda w: w.astype(BF16).reshape(DEPTH, D_MODEL, D_FF // FFN_TN, 1, FFN_TN)
    w_gu_b = jnp.concatenate([tiles(w_gate), tiles(w_up)], axis=3).reshape(DEPTH, D_MODEL, 2 * D_FF)

    for l in range(DEPTH):
        p = norm_matmul(h, norm_mix_g[l][None], w_in_big, l, 2048)
        q, k, v = mla_prep(p, cosp, sinp, q_lat_g[l][None], kv_lat_g[l][None], wq, wkv, l,
                           _rope_gain_rows(q_norm_g[l]), _rope_gain_rows(k_norm_g[l]))
        att = attention(q, k, v)

        z, u = hyena_pre(p, conv_w[l], conv_b[l][None])
        hf, hb = hyena_filter(
            zt, aux,
            jnp.pad(filt_w1[l].T, ((0, 0), (0, epad))), col(filt_b1[l]), col(filt_freq1[l]),
            filt_w2[l].T, col(filt_b2[l]), col(filt_freq2[l]),
            filt_w3[l].astype(BF16), deltas)
        kf = filter_spectrum(hf, hb, fa, fb)
        y = long_conv(z, kf, fa, fb, ga)

        h = mix_matmul_residual(att, y, z, u, hy_skip[l][None], attn_out_g[l][None], hy_out_g[l][None],
                                w_out_b, l, h)

        act = norm_swiglu(h, norm_ffn_g[l][None], w_gu_b, l, FFN_TN)
        h = matmul_residual(act, w_down_b, l, h)

    return h[N_META:T][None]
```
